```python
import math
import jax, jax.numpy as jnp
from jax import lax
import numpy as np

D_MODEL = 2048
BATCH = 2
SEQ = 8192
DEPTH = 4

GRID_W = 64
CTX_LEN = 256
N_MIXERS = 3
N_GLA = (DEPTH + 2) // 3
N_CONV = (DEPTH + 1) // 3
N_DIFF = DEPTH // 3
N_MOD = 6
MLP_HIDDEN = 4 * D_MODEL
EPS = 1e-6
GLA_HEADS = 4
GLA_DK = D_MODEL // 2 // GLA_HEADS
GLA_DV = D_MODEL // GLA_HEADS
GLA_HK = GLA_HEADS * GLA_DK
GLA_HV = GLA_HEADS * GLA_DV
GLA_RANK = 16
GLA_GATE_NORM = 16.0
GLA_CHUNK = 64
CONV_WIDTH = 31
DIFF_HEADS = D_MODEL // 256
DIFF_DK = 128
DIFF_DV = 2 * DIFF_DK
Q_BLOCK = 128
ROPE_BASE = 10000.0
ROPE_AXIS_DIM = DIFF_DK // 2

kernel_name = "hybrid_gla_conformer_diffattn_dit_trunk"

F32 = jnp.float32


def rmsnorm(x, g):
    xf = x.astype(F32)
    y = xf * lax.rsqrt(jnp.mean(xf * xf, axis=-1, keepdims=True) + EPS)
    return (y * g.astype(F32)).astype(x.dtype)


def layernorm(x, g, b):
    xf = x.astype(F32)
    mu = jnp.mean(xf, axis=-1, keepdims=True)
    xc = xf - mu
    y = xc * lax.rsqrt(jnp.mean(xc * xc, axis=-1, keepdims=True) + EPS)
    return (y * g.astype(F32) + b.astype(F32)).astype(x.dtype)


def modulate(h, shift, scale):
    return h * (1.0 + scale) + shift


def squared_relu_mlp(u, w1, w2):
    return jnp.square(jax.nn.relu(u @ w1)) @ w2


def gla_chunk_scan(q, k, v, log_a, s0):
    B, T, H, DK = q.shape
    DV = v.shape[-1]
    N = T // GLA_CHUNK
    dt = q.dtype

    def to_chunks(t):
        return t.reshape(B, N, GLA_CHUNK, H, t.shape[-1]).transpose(1, 0, 3, 2, 4)

    qc, kc, vc, gc = (to_chunks(t) for t in (q, k, v, log_a))
    b = jnp.cumsum(gc.astype(F32), axis=3)
    b_last = b[:, :, :, -1:, :]
    q_dec = (qc.astype(F32) * jnp.exp(b)).astype(dt)
    k_inv = (kc.astype(F32) * jnp.exp(-b)).astype(dt)
    k_end = (kc.astype(F32) * jnp.exp(b_last - b)).astype(dt)
    chunk_decay = jnp.exp(b_last[:, :, :, 0, :]).astype(dt)
    mask = jnp.tril(jnp.ones((GLA_CHUNK, GLA_CHUNK), dtype=bool))
    att = jnp.where(mask, jnp.einsum('nbhcd,nbhsd->nbhcs', q_dec, k_inv), 0)
    o_intra = jnp.einsum('nbhcs,nbhse->nbhce', att, vc)

    def step(state, inp):
        q_n, k_n, v_n, dec_n = inp
        o_n = jnp.einsum('bhcd,bhde->bhce', q_n, state)
        state = dec_n[..., None] * state + jnp.einsum('bhsd,bhse->bhde', k_n, v_n)
        return state, o_n

    s_final, o_inter = lax.scan(step, s0, (q_dec, k_end, vc, chunk_decay))
    o = (o_intra + o_inter).transpose(1, 0, 3, 2, 4).reshape(B, T, H, DV)
    return o, s_final


def gla_mixer(u_lat, u_ctx, w_in, wa1_f, wa2_f, ba_f, wa1_b, wa2_b, ba_b,
              norm_g, w_o, with_ctx_out):
    def project(u):
        B, T, _ = u.shape
        q, k, v, g = jnp.split(u @ w_in, [GLA_HK, 2 * GLA_HK, 2 * GLA_HK + GLA_HV], axis=-1)

        def heads(t, d):
            return t.reshape(B, T, GLA_HEADS, d)

        def log_decay(wa1, wa2, ba):
            z = ((u @ wa1) @ wa2 + ba).astype(F32)
            return heads(jax.nn.log_sigmoid(z) / GLA_GATE_NORM, GLA_DK)

        return (heads(q, GLA_DK) * GLA_DK ** -0.5, heads(k, GLA_DK), heads(v, GLA_DV), g,
                log_decay(wa1_f, wa2_f, ba_f), log_decay(wa1_b, wa2_b, ba_b))

    q_l, k_l, v_l, g_l, af_l, ab_l = project(u_lat)
    q_c, k_c, v_c, g_c, af_c, ab_c = project(u_ctx)
    B = u_lat.shape[0]
    s0 = jnp.zeros((B, GLA_HEADS, GLA_DK, GLA_DV), u_lat.dtype)

    def flip(t):
        return jnp.flip(t, axis=1)

    o_cf, s_cf = gla_chunk_scan(q_c, k_c, v_c, af_c, s0)
    o_lf, _ = gla_chunk_scan(q_l, k_l, v_l, af_l, s_cf)
    o_cb, s_cb = gla_chunk_scan(flip(q_c), flip(k_c), flip(v_c), flip(ab_c), s0)
    o_lb, _ = gla_chunk_scan(flip(q_l), flip(k_l), flip(v_l), flip(ab_l), s_cb)

    def out(o, g):
        B, T = o.shape[:2]
        o = rmsnorm(o, norm_g).reshape(B, T, GLA_HV)
        return (o * jax.nn.silu(g)) @ w_o

    y_lat = out(o_lf + flip(o_lb), g_l)
    y_ctx = out(o_cf + flip(o_cb), g_c) if with_ctx_out else None
    return y_lat, y_ctx


def conformer_conv_mixer(u, w1, b1, dw, dwb, ln_g, ln_b, w2, b2):
    a = u @ w1 + b1
    glu = a[..., :D_MODEL] * jax.nn.sigmoid(a[..., D_MODEL:])
    pad = CONV_WIDTH // 2
    dconv = lax.conv_general_dilated(
        glu, dw[:, None, :].astype(glu.dtype), window_strides=(1,),
        padding=((pad, pad),), dimension_numbers=('NWC', 'WIO', 'NWC'),
        feature_group_count=D_MODEL) + dwb
    return jax.nn.silu(layernorm(dconv, ln_g, ln_b)) @ w2 + b2


def axial_rope_tables(rows):
    pos_row = jnp.repeat(jnp.arange(rows, dtype=F32), GRID_W)
    pos_col = jnp.tile(jnp.arange(GRID_W, dtype=F32), rows)
    inv_freq = ROPE_BASE ** (-jnp.arange(0, ROPE_AXIS_DIM, 2, dtype=F32) / ROPE_AXIS_DIM)
    ang_r = pos_row[:, None] * inv_freq
    ang_c = pos_col[:, None] * inv_freq
    ang = jnp.concatenate([ang_r, ang_r, ang_c, ang_c], axis=-1)
    return jnp.cos(ang), jnp.sin(ang)


def apply_axial_rope(x, cos, sin):
    xf = x.astype(F32)
    xr = xf.reshape(*x.shape[:-1], 2, 2, ROPE_AXIS_DIM // 2)
    rot = jnp.concatenate([-xr[..., 1:, :], xr[..., :1, :]], axis=-2).reshape(x.shape)
    c = cos[None, :, None, None, :]
    s = sin[None, :, None, None, :]
    return (xf * c + rot * s).astype(x.dtype)


def diff_attention_mixer(u_lat, u_ctx, w_qkv, lq1, lk1, lq2, lk2, subln_g, w_o,
                         lambda_init, cos, sin, with_ctx_out):
    def project(u):
        B, T, _ = u.shape
        q, k, v = jnp.split(u @ w_qkv, 3, axis=-1)
        return (q.reshape(B, T, DIFF_HEADS, 2, DIFF_DK),
                k.reshape(B, T, DIFF_HEADS, 2, DIFF_DK),
                v.reshape(B, T, DIFF_HEADS, DIFF_DV))

    q_l, k_l, v_l = project(u_lat)
    q_l = apply_axial_rope(q_l, cos, sin)
    k_l = apply_axial_rope(k_l, cos, sin)
    q_c, k_c, v_c = project(u_ctx)
    lam = (jnp.exp(jnp.sum(lq1.astype(F32) * lk1.astype(F32)))
           - jnp.exp(jnp.sum(lq2.astype(F32) * lk2.astype(F32))) + lambda_init)

    def attend(q, keys, vals):
        s = jnp.einsum('bqhjd,bkhjd->bhjqk', q, keys,
                       preferred_element_type=F32) * DIFF_DK ** -0.5
        p = jax.nn.softmax(s, axis=-1)
        a = (p[:, :, 0] - lam * p[:, :, 1]).astype(vals.dtype)
        return jnp.einsum('bhqk,bkhe->bqhe', a, vals)

    k_all = jnp.concatenate([k_c, k_l], axis=1)
    v_all = jnp.concatenate([v_c, v_l], axis=1)
    B, T = u_lat.shape[:2]
    nb = T // Q_BLOCK
    q_blocks = q_l.reshape(B, nb, Q_BLOCK, DIFF_HEADS, 2, DIFF_DK).transpose(1, 0, 2, 3, 4, 5)
    o_l = lax.map(lambda qb: attend(qb, k_all, v_all), q_blocks)
    o_l = o_l.transpose(1, 0, 2, 3, 4).reshape(B, T, DIFF_HEADS, DIFF_DV)

    def out(o):
        B, T = o.shape[:2]
        o = rmsnorm(o, subln_g) * (1.0 - lambda_init)
        return o.reshape(B, T, DIFF_HEADS * DIFF_DV) @ w_o

    y_lat = out(o_l)
    y_ctx = out(attend(q_c, k_c, v_c)) if with_ctx_out else None
    return y_lat, y_ctx


def setup_inputs(seed: int = 0) -> dict:
    key = jax.random.key(seed)
    ks = iter(jax.random.split(key, 40))
    D = D_MODEL

    def nrm(shape, scale):
        return jax.random.normal(next(ks), shape, F32) * scale

    def gain(shape):
        return 1.0 + nrm(shape, 0.02)

    def bias(shape):
        return nrm(shape, 0.02)

    return {
        "x": nrm((BATCH, SEQ, D), 1.0),
        "c": nrm((BATCH, D), 1.0),
        "ctx": nrm((BATCH, CTX_LEN, D), 1.0),
        "c_ctx": nrm((D,), 1.0),
        "mod_w": nrm((DEPTH, D, N_MOD * D), D ** -0.5),
        "mod_b": bias((DEPTH, N_MOD * D)),
        "norm_mix_g": gain((DEPTH, D)),
        "norm_mlp_g": gain((DEPTH, D)),
        "mlp_w1": nrm((DEPTH, D, MLP_HIDDEN), D ** -0.5),
        "mlp_w2": nrm((DEPTH, MLP_HIDDEN, D), MLP_HIDDEN ** -0.5),
        "gla_w_in": nrm((N_GLA, D, 2 * GLA_HK + 2 * GLA_HV), D ** -0.5),
        "gla_wa1_f": nrm((N_GLA, D, GLA_RANK), D ** -0.5),
        "gla_wa2_f": nrm((N_GLA, GLA_RANK, GLA_HK), GLA_RANK ** -0.5),
        "gla_ba_f": bias((N_GLA, GLA_HK)),
        "gla_wa1_b": nrm((N_GLA, D, GLA_RANK), D ** -0.5),
        "gla_wa2_b": nrm((N_GLA, GLA_RANK, GLA_HK), GLA_RANK ** -0.5),
        "gla_ba_b": bias((N_GLA, GLA_HK)),
        "gla_norm_g": gain((N_GLA, GLA_DV)),
        "gla_w_o": nrm((N_GLA, GLA_HV, D), GLA_HV ** -0.5),
        "conv_w1": nrm((N_CONV, D, 2 * D), D ** -0.5),
        "conv_b1": bias((N_CONV, 2 * D)),
        "conv_dw": nrm((N_CONV, CONV_WIDTH, D), CONV_WIDTH ** -0.5),
        "conv_dwb": bias((N_CONV, D)),
        "conv_ln_g": gain((N_CONV, D)),
        "conv_ln_b": bias((N_CONV, D)),
        "conv_w2": nrm((N_CONV, D, D), D ** -0.5),
        "conv_b2": bias((N_CONV, D)),
        "diff_w_qkv": nrm((N_DIFF, D, 3 * DIFF_HEADS * 2 * DIFF_DK), D ** -0.5),
        "diff_lq1": nrm((N_DIFF, DIFF_DK), 0.1),
        "diff_lk1": nrm((N_DIFF, DIFF_DK), 0.1),
        "diff_lq2": nrm((N_DIFF, DIFF_DK), 0.1),
        "diff_lk2": nrm((N_DIFF, DIFF_DK), 0.1),
        "diff_subln_g": gain((N_DIFF, DIFF_DV)),
        "diff_w_o": nrm((N_DIFF, DIFF_HEADS * DIFF_DV, D), (DIFF_HEADS * DIFF_DV) ** -0.5),
        "final_g": gain((D,)),
    }


def reference(x, c, ctx, c_ctx, mod_w, mod_b, norm_mix_g, norm_mlp_g, mlp_w1, mlp_w2,
              gla_w_in, gla_wa1_f, gla_wa2_f, gla_ba_f, gla_wa1_b, gla_wa2_b, gla_ba_b,
              gla_norm_g, gla_w_o,
              conv_w1, conv_b1, conv_dw, conv_dwb, conv_ln_g, conv_ln_b, conv_w2, conv_b2,
              diff_w_qkv, diff_lq1, diff_lk1, diff_lq2, diff_lk2, diff_subln_g, diff_w_o,
              final_g):
    n_tokens = x.shape[1]
    rows = n_tokens // GRID_W
    cos, sin = axial_rope_tables(rows)
    sc = jax.nn.silu(c)
    scc = jax.nn.silu(c_ctx)
    h_lat, h_ctx = x, ctx
    for i in range(DEPTH):
        with_ctx = i < DEPTH - 1
        mod_lat = (sc @ mod_w[i] + mod_b[i])[:, None, :]
        mod_ctx = (scc @ mod_w[i] + mod_b[i])[None, None, :]
        sh1, s1, g1, sh2, s2, g2 = jnp.split(mod_lat, N_MOD, axis=-1)
        csh1, cs1, cg1, csh2, cs2, cg2 = jnp.split(mod_ctx, N_MOD, axis=-1)

        u_lat = modulate(rmsnorm(h_lat, norm_mix_g[i]), sh1, s1)
        u_ctx = modulate(rmsnorm(h_ctx, norm_mix_g[i]), csh1, cs1)
        kind, j = i % N_MIXERS, i // N_MIXERS
        if kind == 0:
            y_lat, y_ctx = gla_mixer(u_lat, u_ctx, gla_w_in[j], gla_wa1_f[j], gla_wa2_f[j],
                                     gla_ba_f[j], gla_wa1_b[j], gla_wa2_b[j], gla_ba_b[j],
                                     gla_norm_g[j], gla_w_o[j], with_ctx)
        elif kind == 1:
            conv_args = (conv_w1[j], conv_b1[j], conv_dw[j], conv_dwb[j], conv_ln_g[j],
                         conv_ln_b[j], conv_w2[j], conv_b2[j])
            y_lat = conformer_conv_mixer(u_lat, *conv_args)
            y_ctx = conformer_conv_mixer(u_ctx, *conv_args) if with_ctx else None
        else:
            lambda_init = 0.8 - 0.6 * math.exp(-0.3 * i)
            y_lat, y_ctx = diff_attention_mixer(u_lat, u_ctx, diff_w_qkv[j], diff_lq1[j],
                                                diff_lk1[j], diff_lq2[j], diff_lk2[j],
                                                diff_subln_g[j], diff_w_o[j], lambda_init,
                                                cos, sin, with_ctx)
        h_lat = h_lat + g1 * y_lat
        h_lat = h_lat + g2 * squared_relu_mlp(
            modulate(rmsnorm(h_lat, norm_mlp_g[i]), sh2, s2), mlp_w1[i], mlp_w2[i])
        if with_ctx:
            h_ctx = h_ctx + cg1 * y_ctx
            h_ctx = h_ctx + cg2 * squared_relu_mlp(
                modulate(rmsnorm(h_ctx, norm_mlp_g[i]), csh2, cs2), mlp_w1[i], mlp_w2[i])
    return rmsnorm(h_lat, final_g)
```

```python
import functools
import math

import jax
import jax.numpy as jnp
from jax import lax
from jax.experimental import pallas as pl
from jax.experimental.pallas import tpu as pltpu

F32 = jnp.float32
ACT = jnp.bfloat16
EPS = 1e-6
N_MOD = 6
GLA_HEADS = 4
GLA_RANK = 16
GLA_GATE_NORM = 16.0
GLA_CHUNK = 64
CONV_WIDTH = 31
CONV_HALO = 16
DIFF_DK = 128
DIFF_DV = 256
GRID_W = 64
ROPE_BASE = 10000.0
LANE = 128
MOD_ROWS = 8
VMEM_LIMIT = 56 * 2 ** 20


def _params(n_axes):
    return pltpu.CompilerParams(dimension_semantics=("arbitrary",) * n_axes,
                                vmem_limit_bytes=VMEM_LIMIT)


def _dot(a, b):
    return jnp.dot(a, b, preferred_element_type=F32)


def _dot_nt(a, b):
    return lax.dot_general(a, b, (((1,), (1,)), ((), ())), preferred_element_type=F32)


def _dot_tn(a, b):
    return lax.dot_general(a, b, (((0,), (0,)), ((), ())), preferred_element_type=F32)


def _sigmoid(x):
    return 1.0 / (1.0 + jnp.exp(-x))


def _rms(x):
    return x * lax.rsqrt(jnp.mean(x * x, axis=-1, keepdims=True) + EPS)


def _row_tile(L, n_ctx, cap):
    best = n_ctx
    for m in range(1, L // n_ctx + 1):
        t = m * n_ctx
        if t <= cap and L % t == 0:
            best = t
    return best


def _col_tile(n, cap):
    best = LANE
    t = LANE
    while t <= min(n, cap):
        if n % t == 0:
            best = t
        t += LANE
    return best


def _mod_pick(ml_ref, mc_ref, k, D, ctx_row):
    lat = ml_ref[:, k * D:(k + 1) * D]
    ctx = mc_ref[:, k * D:(k + 1) * D]
    return jnp.where(ctx_row, ctx, lat)


def _ctx_rows(tm, n_ctx, first_tile):
    rows = lax.broadcasted_iota(jnp.int32, (tm, 1), 0)
    return jnp.logical_and(first_tile, rows < n_ctx)


def _norm_mod(h, g_ref, ml_ref, mc_ref, k_shift, k_scale, ctx_row):
    D = h.shape[-1]
    y = _rms(h) * g_ref[...]
    shift = _mod_pick(ml_ref, mc_ref, k_shift, D, ctx_row)
    scale = _mod_pick(ml_ref, mc_ref, k_scale, D, ctx_row)
    return y * (1.0 + scale) + shift


def _mod_kernel(c_ref, w_ref, b_ref, o_ref):
    c = c_ref[...]
    sc = (c * _sigmoid(c)).astype(ACT)
    o_ref[...] = _dot(sc, w_ref[...].astype(ACT)) + b_ref[...]


def _mod_vectors(cvec, mod_w, mod_b):
    depth, D, N = mod_w.shape
    tn = _col_tile(N, 1024)
    return pl.pallas_call(
        _mod_kernel,
        grid=(depth, N // tn),
        in_specs=[pl.BlockSpec((MOD_ROWS, D), lambda l, j: (0, 0)),
                  pl.BlockSpec((None, D, tn), lambda l, j: (l, 0, j)),
                  pl.BlockSpec((None, 1, tn), lambda l, j: (l, 0, j))],
        out_specs=pl.BlockSpec((None, MOD_ROWS, tn), lambda l, j: (l, 0, j)),
        out_shape=jax.ShapeDtypeStruct((depth, MOD_ROWS, N), F32),
        compiler_params=_params(2),
        name="mod_vectors",
    )(cvec, mod_w, mod_b.reshape(depth, 1, N))


def _rope_tile(acc, cos, sin_lo, sin_hi):
    up = pltpu.roll(acc, LANE - DIFF_DK // 4, axis=1)
    down = pltpu.roll(acc, DIFF_DK // 4, axis=1)
    return acc * cos + up * sin_lo + down * sin_hi


def _proj_kernel(*refs, mode, tpb, n_ctx, n_rope_tiles, q_tiles, q_scale):
    if mode == "glu":
        h_ref, g_ref, ml_ref, mc_ref, wa_ref, wg_ref, ba_ref, bg_ref, o_ref, u_scr = refs
    elif mode == "rope":
        h_ref, g_ref, ml_ref, mc_ref, w_ref, cos_ref, slo_ref, shi_ref, o_ref, u_scr = refs
    else:
        h_ref, g_ref, ml_ref, mc_ref, w_ref, o_ref, u_scr = refs
    i = pl.program_id(0)
    j = pl.program_id(1)
    tm = h_ref.shape[0]

    @pl.when(j == 0)
    def _():
        ctx_row = _ctx_rows(tm, n_ctx, i % tpb == 0)
        u_scr[...] = _norm_mod(h_ref[...], g_ref, ml_ref, mc_ref, 0, 1, ctx_row).astype(ACT)

    u = u_scr[...]
    if mode == "glu":
        a = _dot(u, wa_ref[...]) + ba_ref[...]
        g = _dot(u, wg_ref[...]) + bg_ref[...]
        o_ref[...] = (a * _sigmoid(g)).astype(o_ref.dtype)
    elif mode == "rope":
        acc = _dot(u, w_ref[...])

        @pl.when(j < n_rope_tiles)
        def _():
            scale = jnp.where(j < q_tiles, q_scale, 1.0)
            cos, slo, shi = cos_ref[...], slo_ref[...], shi_ref[...]
            for s in range(acc.shape[1] // LANE):
                seg = acc[:, s * LANE:(s + 1) * LANE]
                o_ref[:, s * LANE:(s + 1) * LANE] = (
                    _rope_tile(seg, cos, slo, shi) * scale).astype(o_ref.dtype)

        @pl.when(j >= n_rope_tiles)
        def _():
            o_ref[...] = acc.astype(o_ref.dtype)
    else:
        o_ref[...] = _dot(u, w_ref[...]).astype(o_ref.dtype)


def _project(hs, g, ml, mc, w, *, L, n_ctx, mode="plain", bias=None, rope=None,
             n_rope_cols=0, n_q_cols=0, q_scale=1.0):
    M, D = hs.shape
    N = w.shape[1]
    n_out = N // 2 if mode == "glu" else N
    tm = _row_tile(L, n_ctx, 768)
    tpb = L // tm
    tn = _col_tile(n_out, 1024)
    if mode == "rope":
        while n_rope_cols % tn or n_q_cols % tn:
            tn -= LANE
    nj = n_out // tn
    row = lambda i, j: (i, 0)
    fixed = lambda i, j: (0, 0)
    in_specs = [pl.BlockSpec((tm, D), row),
                pl.BlockSpec((1, D), fixed),
                pl.BlockSpec((None, 1, N_MOD * D), lambda i, j: (i // tpb, 0, 0)),
                pl.BlockSpec((1, N_MOD * D), fixed)]
    args = [hs, g.reshape(1, D), ml, mc]
    if mode == "glu":
        in_specs += [pl.BlockSpec((D, tn), lambda i, j: (0, j)),
                     pl.BlockSpec((D, tn), lambda i, j: (0, j + nj)),
                     pl.BlockSpec((1, tn), lambda i, j: (0, j)),
                     pl.BlockSpec((1, tn), lambda i, j: (0, j + nj))]
        b2 = bias.reshape(1, N)
        args += [w, w, b2, b2]
    else:
        in_specs.append(pl.BlockSpec((D, tn), lambda i, j: (0, j)))
        args.append(w)
        if mode == "rope":
            tab = pl.BlockSpec((tm, LANE), lambda i, j: (i % tpb, 0))
            in_specs += [tab, tab, tab]
            args += list(rope)
    kern = functools.partial(_proj_kernel, mode=mode, tpb=tpb, n_ctx=n_ctx,
                             n_rope_tiles=n_rope_cols // tn, q_tiles=n_q_cols // tn,
                             q_scale=q_scale)
    return pl.pallas_call(
        kern,
        grid=(M // tm, nj),
        in_specs=in_specs,
        out_specs=pl.BlockSpec((tm, tn), lambda i, j: (i, j)),
        out_shape=jax.ShapeDtypeStruct((M, n_out), ACT),
        scratch_shapes=[pltpu.VMEM((tm, D), ACT)],
        compiler_params=_params(2),
        name="project_" + mode,
    )(*args)


def _mlp_kernel(h_ref, g_ref, ml_ref, mc_ref, w1_ref, w2_ref, o_ref, u_scr, acc_scr,
                *, tpb, n_ctx):
    i = pl.program_id(0)
    j = pl.program_id(1)
    tm, D = h_ref.shape
    ctx_row = _ctx_rows(tm, n_ctx, i % tpb == 0)

    @pl.when(j == 0)
    def _():
        u_scr[...] = _norm_mod(h_ref[...], g_ref, ml_ref, mc_ref, 3, 4, ctx_row).astype(ACT)
        acc_scr[...] = jnp.zeros_like(acc_scr)

    hid = jnp.maximum(_dot(u_scr[...], w1_ref[...]), 0.0)
    acc_scr[...] += _dot((hid * hid).astype(ACT), w2_ref[...])

    @pl.when(j == pl.num_programs(1) - 1)
    def _():
        gate = _mod_pick(ml_ref, mc_ref, 5, D, ctx_row)
        o_ref[...] = h_ref[...] + gate * acc_scr[...]


def _mlp(hs, g, ml, mc, w1, w2, *, L, n_ctx):
    M, D = hs.shape
    Hd = w1.shape[1]
    tm = _row_tile(L, n_ctx, 768)
    tpb = L // tm
    th = _col_tile(Hd, 512)
    row = lambda i, j: (i, 0)
    fixed = lambda i, j: (0, 0)
    return pl.pallas_call(
        functools.partial(_mlp_kernel, tpb=tpb, n_ctx=n_ctx),
        grid=(M // tm, Hd // th),
        in_specs=[pl.BlockSpec((tm, D), row),
                  pl.BlockSpec((1, D), fixed),
                  pl.BlockSpec((None, 1, N_MOD * D), lambda i, j: (i // tpb, 0, 0)),
                  pl.BlockSpec((1, N_MOD * D), fixed),
                  pl.BlockSpec((D, th), lambda i, j: (0, j)),
                  pl.BlockSpec((th, D), lambda i, j: (j, 0))],
        out_specs=pl.BlockSpec((tm, D), row),
        out_shape=jax.ShapeDtypeStruct((M, D), F32),
        scratch_shapes=[pltpu.VMEM((tm, D), ACT), pltpu.VMEM((tm, D), F32)],
        compiler_params=_params(2),
        name="mlp",
    )(hs, g.reshape(1, D), ml, mc, w1, w2)


def _out_kernel(*refs, mode, tpb, n_ctx, n_heads):
    if mode == "gla":
        h_ref, ml_ref, mc_ref, w_ref, of_ref, ob_ref, gt_ref, ng_ref, o_ref = refs
    else:
        h_ref, ml_ref, mc_ref, w_ref, a_ref, o_ref = refs
    i = pl.program_id(0)
    tm, D = h_ref.shape
    ctx_row = _ctx_rows(tm, n_ctx, i % tpb == 0)
    if mode == "gla":
        o = of_ref[...].astype(F32) + ob_ref[...].astype(F32)
        dv = o.shape[1] // n_heads
        ng = ng_ref[...]
        o = jnp.concatenate([_rms(o[:, hh * dv:(hh + 1) * dv]) * ng for hh in range(n_heads)],
                            axis=1)
        gt = gt_ref[...].astype(F32)
        a = (o * (gt * _sigmoid(gt))).astype(ACT)
    else:
        a = a_ref[...]
    gate = _mod_pick(ml_ref, mc_ref, 2, D, ctx_row)
    o_ref[...] = h_ref[...] + gate * _dot(a, w_ref[...])


def _out_residual(hs, ml, mc, w, *, L, n_ctx, a=None, gla=None):
    M, D = hs.shape
    K = w.shape[0]
    tm = _row_tile(L, n_ctx, 256)
    tpb = L // tm
    row = lambda i: (i, 0)
    fixed = lambda i: (0, 0)
    in_specs = [pl.BlockSpec((tm, D), row),
                pl.BlockSpec((None, 1, N_MOD * D), lambda i: (i // tpb, 0, 0)),
                pl.BlockSpec((1, N_MOD * D), fixed),
                pl.BlockSpec((K, D), fixed)]
    args = [hs, ml, mc, w]
    if gla is not None:
        o_f, o_b, qkvg, norm_g = gla
        g_blk = qkvg.shape[1] // K - 1
        in_specs += [pl.BlockSpec((tm, K), row), pl.BlockSpec((tm, K), row),
                     pl.BlockSpec((tm, K), lambda i: (i, g_blk)),
                     pl.BlockSpec((1, norm_g.shape[0]), fixed)]
        args += [o_f, o_b, qkvg, norm_g.reshape(1, -1)]
        mode = "gla"
    else:
        in_specs.append(pl.BlockSpec((tm, K), row))
        args.append(a)
        mode = "plain"
    return pl.pallas_call(
        functools.partial(_out_kernel, mode=mode, tpb=tpb, n_ctx=n_ctx, n_heads=GLA_HEADS),
        grid=(M // tm,),
        in_specs=in_specs,
        out_specs=pl.BlockSpec((tm, D), row),
        out_shape=jax.ShapeDtypeStruct((M, D), F32),
        compiler_params=_params(1),
        name="out_" + mode,
    )(*args)


def _gate_kernel(h_ref, g_ref, ml_ref, mc_ref, wa1_ref, wa2_ref, ba_ref, o_ref, *, tpb, n_ctx):
    i = pl.program_id(0)
    tm = h_ref.shape[0]
    ctx_row = _ctx_rows(tm, n_ctx, i % tpb == 0)
    u = _norm_mod(h_ref[...], g_ref, ml_ref, mc_ref, 0, 1, ctx_row).astype(ACT)
    t = _dot(u, wa1_ref[...]).astype(ACT)
    z = _dot(t, wa2_ref[...]) + ba_ref[...]
    log_sig = jnp.minimum(z, 0.0) - jnp.log(1.0 + jnp.exp(-jnp.abs(z)))
    o_ref[...] = log_sig / GLA_GATE_NORM


def _gla_gates(hs, g, ml, mc, wa1, wa2, ba, *, L, n_ctx):
    M, D = hs.shape
    R, N = wa2.shape
    tm = _row_tile(L, n_ctx, 768)
    tpb = L // tm
    row = lambda i: (i, 0)
    fixed = lambda i: (0, 0)
    return pl.pallas_call(
        functools.partial(_gate_kernel, tpb=tpb, n_ctx=n_ctx),
        grid=(M // tm,),
        in_specs=[pl.BlockSpec((tm, D), row),
                  pl.BlockSpec((1, D), fixed),
                  pl.BlockSpec((None, 1, N_MOD * D), lambda i: (i // tpb, 0, 0)),
                  pl.BlockSpec((1, N_MOD * D), fixed),
                  pl.BlockSpec((D, R), fixed),
                  pl.BlockSpec((R, N), fixed),
                  pl.BlockSpec((1, N), fixed)],
        out_specs=pl.BlockSpec((tm, N), row),
        out_shape=jax.ShapeDtypeStruct((M, N), F32),
        compiler_params=_params(1),
        name="gla_gates",
    )(hs, g.reshape(1, D), ml, mc, wa1, wa2, ba.reshape(1, N))


def _gla_chunk(q_ref, k_ref, v_ref, g_ref, o_ref, s_scr, c0, mask, last_row, q_scale):
    C = GLA_CHUNK
    rows = pl.ds(c0, C)
    g = g_ref[rows, :]
    tri = mask.astype(ACT)
    g_hi = g.astype(ACT)
    rest = g - g_hi.astype(F32)
    g_mid = rest.astype(ACT)
    g_lo = (rest - g_mid.astype(F32)).astype(ACT)
    b = _dot(tri, g_hi) + _dot(tri, g_mid) + _dot(tri, g_lo)
    b_last = b[last_row:last_row + 1, :]
    q = q_ref[rows, :].astype(F32) * q_scale
    k = k_ref[rows, :].astype(F32)
    q_dec = (q * jnp.exp(b)).astype(ACT)
    k_inv = (k * jnp.exp(-b)).astype(ACT)
    k_end = (k * jnp.exp(b_last - b)).astype(ACT)
    att = jnp.where(mask, _dot_nt(q_dec, k_inv), 0.0).astype(ACT)
    v = v_ref[rows, :]
    state = s_scr[...]
    o = _dot(att, v) + _dot_nt(q_dec, state.astype(ACT))
    o_ref[rows, :] = o.astype(o_ref.dtype)
    s_scr[...] = state * jnp.exp(b_last) + _dot_tn(v, k_end)


def _gla_kernel(qf_ref, kf_ref, vf_ref, gf_ref, qb_ref, kb_ref, vb_ref, gb_ref,
                of_ref, ob_ref, sf_scr, sb_scr, *, q_scale):
    @pl.when(pl.program_id(2) == 0)
    def _():
        sf_scr[...] = jnp.zeros_like(sf_scr)
        sb_scr[...] = jnp.zeros_like(sb_scr)

    C = GLA_CHUNK
    n_chunks = qf_ref.shape[0] // C
    r = lax.broadcasted_iota(jnp.int32, (C, C), 0)
    c = lax.broadcasted_iota(jnp.int32, (C, C), 1)
    for n in range(n_chunks):
        _gla_chunk(qf_ref, kf_ref, vf_ref, gf_ref, of_ref, sf_scr, n * C, r >= c, C - 1, q_scale)
        m = n_chunks - 1 - n
        _gla_chunk(qb_ref, kb_ref, vb_ref, gb_ref, ob_ref, sb_scr, m * C, r <= c, 0, q_scale)


def _gla_scan(qkvg, la, *, B, L, n_ctx, dk, dv):
    M = qkvg.shape[0]
    H = GLA_HEADS
    blk = n_ctx
    nb = L // blk
    kq = (H * dk) // dk
    kv = (2 * H * dk) // dv

    def fwd(n):
        return n

    def bwd(n):
        return jnp.where(n == 0, 0, nb - n)

    def spec(width, off, order):
        return pl.BlockSpec((blk, width), lambda b, h, n: (b * nb + order(n), off + h))

    out_spec = lambda order: pl.BlockSpec((blk, dv), lambda b, h, n: (b * nb + order(n), h))
    return pl.pallas_call(
        functools.partial(_gla_kernel, q_scale=dk ** -0.5),
        grid=(B, H, nb),
        in_specs=[spec(dk, 0, fwd), spec(dk, kq, fwd), spec(dv, kv, fwd), spec(dk, 0, fwd),
                  spec(dk, 0, bwd), spec(dk, kq, bwd), spec(dv, kv, bwd), spec(dk, kq, bwd)],
        out_specs=[out_spec(fwd), out_spec(bwd)],
        out_shape=[jax.ShapeDtypeStruct((M, H * dv), ACT)] * 2,
        scratch_shapes=[pltpu.VMEM((dv, dk), F32), pltpu.VMEM((dv, dk), F32)],
        compiler_params=_params(3),
        name="gla_scan",
    )(qkvg, qkvg, qkvg, la, qkvg, qkvg, qkvg, la)


def _conv_kernel(h_ref, ml_ref, mc_ref, x_ref, prev_ref, next_ref, dw_ref, dwb_ref,
                 lng_ref, lnb_ref, w_ref, b_ref, o_ref, xw_scr, y_scr, *, tpb):
    i = pl.program_id(0)
    tm, D = h_ref.shape
    t = i % tpb
    has_prev = t > 1
    has_next = jnp.logical_and(t > 0, t < tpb - 1)
    xw_scr[0:CONV_HALO, :] = jnp.where(has_prev, prev_ref[...].astype(F32), 0.0)
    xw_scr[CONV_HALO:CONV_HALO + tm, :] = x_ref[...].astype(F32)
    xw_scr[CONV_HALO + tm:, :] = jnp.where(has_next, next_ref[...].astype(F32), 0.0)
    first = CONV_HALO - CONV_WIDTH // 2
    rb = min(tm, 128)

    def col_body(cb, carry):
        cols = pl.ds(pl.multiple_of(cb * LANE, LANE), LANE)
        for r0 in range(0, tm, rb):
            acc = jnp.zeros((rb, LANE), F32)
            for k in range(CONV_WIDTH):
                acc = acc + xw_scr[pl.ds(r0 + first + k, rb), cols] * dw_ref[pl.ds(k, 1), cols]
            y_scr[pl.ds(r0, rb), cols] = acc + dwb_ref[:, cols]
        return carry

    lax.fori_loop(0, D // LANE, col_body, 0)
    y = y_scr[...]
    yc = y - jnp.mean(y, axis=-1, keepdims=True)
    yn = yc * lax.rsqrt(jnp.mean(yc * yc, axis=-1, keepdims=True) + EPS)
    yn = yn * lng_ref[...] + lnb_ref[...]
    a = (yn * _sigmoid(yn)).astype(ACT)
    gate = _mod_pick(ml_ref, mc_ref, 2, D, t == 0)
    o_ref[...] = h_ref[...] + gate * (_dot(a, w_ref[...]) + b_ref[...])


def _conv_out(hs, ml, mc, glu, dw, dwb, ln_g, ln_b, w2, b2, *, L, n_ctx):
    M, D = hs.shape
    tm = n_ctx
    tpb = L // tm
    hb = tm // CONV_HALO
    n_halo = M // CONV_HALO
    row = lambda i: (i, 0)
    fixed = lambda i: (0, 0)
    vec = pl.BlockSpec((1, D), fixed)
    return pl.pallas_call(
        functools.partial(_conv_kernel, tpb=tpb),
        grid=(M // tm,),
        in_specs=[pl.BlockSpec((tm, D), row),
                  pl.BlockSpec((None, 1, N_MOD * D), lambda i: (i // tpb, 0, 0)),
                  pl.BlockSpec((1, N_MOD * D), fixed),
                  pl.BlockSpec((tm, D), row),
                  pl.BlockSpec((CONV_HALO, D), lambda i: (jnp.maximum(i * hb - 1, 0), 0)),
                  pl.BlockSpec((CONV_HALO, D), lambda i: (jnp.minimum((i + 1) * hb, n_halo - 1), 0)),
                  pl.BlockSpec((CONV_WIDTH, D), fixed),
                  vec, vec, vec,
                  pl.BlockSpec((D, D), fixed),
                  vec],
        out_specs=pl.BlockSpec((tm, D), row),
        out_shape=jax.ShapeDtypeStruct((M, D), F32),
        scratch_shapes=[pltpu.VMEM((tm + 2 * CONV_HALO, D), F32), pltpu.VMEM((tm, D), F32)],
        compiler_params=_params(1),
        name="conv_out",
    )(hs, ml, mc, glu, glu, glu, dw, dwb.reshape(1, D), ln_g.reshape(1, D), ln_b.reshape(1, D),
      w2, b2.reshape(1, D))


def _attn_kernel(q_ref, k_ref, v_ref, lq1_ref, lk1_ref, lq2_ref, lk2_ref, sg_ref, o_ref,
                 acc_scr, m_scr, l_scr, *, n_ctx, tk, lambda_init):
    qi = pl.program_id(2)
    L = k_ref.shape[0]
    acc_scr[...] = jnp.zeros_like(acc_scr)
    m_scr[...] = jnp.full_like(m_scr, -1e30)
    l_scr[...] = jnp.zeros_like(l_scr)

    def step(k0, size):
        v = v_ref[pl.ds(k0, size), :]
        for j in range(2):
            lanes = slice(j * DIFF_DK, (j + 1) * DIFF_DK)
            s = _dot_nt(q_ref[:, lanes], k_ref[pl.ds(k0, size), lanes])
            m_prev = m_scr[j]
            m_new = jnp.maximum(m_prev, jnp.max(s, axis=-1, keepdims=True))
            alpha = jnp.exp(m_prev - m_new)
            p = jnp.exp(s - m_new)
            l_scr[j] = alpha * l_scr[j] + jnp.sum(p, axis=-1, keepdims=True)
            acc_scr[j] = alpha * acc_scr[j] + _dot(p.astype(ACT), v)
            m_scr[j] = m_new

    step(0, n_ctx)

    @pl.when(qi > 0)
    def _():
        def body(t, carry):
            step(pl.multiple_of(n_ctx + t * tk, n_ctx), tk)
            return carry
        lax.fori_loop(0, (L - n_ctx) // tk, body, 0)

    lam = (jnp.exp(jnp.sum(lq1_ref[...] * lk1_ref[...], axis=-1, keepdims=True))
           - jnp.exp(jnp.sum(lq2_ref[...] * lk2_ref[...], axis=-1, keepdims=True)) + lambda_init)
    o = acc_scr[0] / l_scr[0] - lam * (acc_scr[1] / l_scr[1])
    o_ref[...] = (_rms(o) * sg_ref[...] * (1.0 - lambda_init)).astype(o_ref.dtype)


def _diff_attention(qkv, lq1, lk1, lq2, lk2, subln_g, *, B, L, n_ctx, lambda_init):
    M, N = qkv.shape
    width = 2 * DIFF_DK
    H = N // (3 * width)
    tq = n_ctx
    nq = L // tq
    tk = _row_tile(L - n_ctx, n_ctx, 512) if (L - n_ctx) % n_ctx == 0 else n_ctx
    fixed = lambda b, h, i: (0, 0)
    vec = pl.BlockSpec((1, DIFF_DK), fixed)
    return pl.pallas_call(
        functools.partial(_attn_kernel, n_ctx=n_ctx, tk=tk, lambda_init=lambda_init),
        grid=(B, H, nq),
        in_specs=[pl.BlockSpec((tq, width), lambda b, h, i: (b * nq + i, h)),
                  pl.BlockSpec((L, width), lambda b, h, i: (b, H + h)),
                  pl.BlockSpec((L, DIFF_DV), lambda b, h, i: (b, 2 * H + h)),
                  vec, vec, vec, vec,
                  pl.BlockSpec((1, DIFF_DV), fixed)],
        out_specs=pl.BlockSpec((tq, DIFF_DV), lambda b, h, i: (b * nq + i, h)),
        out_shape=jax.ShapeDtypeStruct((M, H * DIFF_DV), ACT),
        scratch_shapes=[pltpu.VMEM((2, tq, DIFF_DV), F32), pltpu.VMEM((2, tq, 1), F32),
                        pltpu.VMEM((2, tq, 1), F32)],
        compiler_params=_params(3),
        name="diff_attention",
    )(qkv, qkv, qkv, lq1.reshape(1, -1), lk1.reshape(1, -1), lq2.reshape(1, -1),
      lk2.reshape(1, -1), subln_g.reshape(1, -1))


def _rope_tables(n_ctx, n_lat):
    axis_dim = DIFF_DK // 2
    rows = n_lat // GRID_W
    pos_row = jnp.repeat(jnp.arange(rows, dtype=F32), GRID_W)
    pos_col = jnp.tile(jnp.arange(GRID_W, dtype=F32), rows)
    inv_freq = ROPE_BASE ** (-jnp.arange(0, axis_dim, 2, dtype=F32) / axis_dim)
    ang_r = pos_row[:, None] * inv_freq
    ang_c = pos_col[:, None] * inv_freq
    ang = jnp.concatenate([ang_r, ang_r, ang_c, ang_c], axis=-1)
    cos = jnp.concatenate([jnp.ones((n_ctx, DIFF_DK), F32), jnp.cos(ang)], axis=0)
    sin = jnp.concatenate([jnp.zeros((n_ctx, DIFF_DK), F32), jnp.sin(ang)], axis=0)
    first_half = (jnp.arange(DIFF_DK) % (axis_dim)) < axis_dim // 2
    return cos, jnp.where(first_half, -sin, 0.0), jnp.where(first_half, 0.0, sin)


def _final_kernel(h_ref, g_ref, o_ref):
    o_ref[...] = _rms(h_ref[...]) * g_ref[...]


def _final_norm(hs, g, *, B, L, n_ctx):
    D = hs.shape[1]
    tm = n_ctx
    skip = n_ctx // tm
    n_lat = L - n_ctx
    return pl.pallas_call(
        _final_kernel,
        grid=(B, n_lat // tm),
        in_specs=[pl.BlockSpec((None, tm, D), lambda b, i: (b, i + skip, 0)),
                  pl.BlockSpec((1, D), lambda b, i: (0, 0))],
        out_specs=pl.BlockSpec((None, tm, D), lambda b, i: (b, i, 0)),
        out_shape=jax.ShapeDtypeStruct((B, n_lat, D), F32),
        compiler_params=_params(2),
        name="final_norm",
    )(hs.reshape(B, L, D), g.reshape(1, D))


def kernel(x, c, ctx, c_ctx, mod_w, mod_b, norm_mix_g, norm_mlp_g, mlp_w1, mlp_w2, gla_w_in, gla_wa1_f, gla_wa2_f, gla_ba_f, gla_wa1_b, gla_wa2_b, gla_ba_b, gla_norm_g, gla_w_o, conv_w1, conv_b1, conv_dw, conv_dwb, conv_ln_g, conv_ln_b, conv_w2, conv_b2, diff_w_qkv, diff_lq1, diff_lk1, diff_lq2, diff_lk2, diff_subln_g, diff_w_o, final_g):
    B, n_lat, D = x.shape
    n_ctx = ctx.shape[1]
    L = n_ctx + n_lat
    depth = mod_w.shape[0]
    assert B + 1 <= MOD_ROWS and n_lat % n_ctx == 0 and n_ctx % GLA_CHUNK == 0
    dims = dict(L=L, n_ctx=n_ctx)

    hs = jnp.concatenate([ctx, x], axis=1).reshape(B * L, D)
    cvec = jnp.zeros((MOD_ROWS, D), F32).at[:B].set(c).at[B].set(c_ctx)
    mods = _mod_vectors(cvec, mod_w, mod_b)
    rope = _rope_tables(n_ctx, n_lat)

    for i in range(depth):
        ml = mods[i, :B][:, None, :]
        mc = mods[i, B:B + 1]
        kind, j = i % 3, i // 3
        if kind == 0:
            hk = gla_wa2_f.shape[2]
            dk, dv = hk // GLA_HEADS, gla_w_o.shape[1] // GLA_HEADS
            qkvg = _project(hs, norm_mix_g[i], ml, mc, gla_w_in[j].astype(ACT), **dims)
            wa1 = jnp.zeros((D, LANE), F32).at[:, :GLA_RANK].set(gla_wa1_f[j])
            wa1 = wa1.at[:, GLA_RANK:2 * GLA_RANK].set(gla_wa1_b[j])
            wa2 = jnp.zeros((LANE, 2 * hk), F32).at[:GLA_RANK, :hk].set(gla_wa2_f[j])
            wa2 = wa2.at[GLA_RANK:2 * GLA_RANK, hk:].set(gla_wa2_b[j])
            ba = jnp.concatenate([gla_ba_f[j], gla_ba_b[j]])
            la = _gla_gates(hs, norm_mix_g[i], ml, mc, wa1.astype(ACT), wa2.astype(ACT), ba, **dims)
            o_f, o_b = _gla_scan(qkvg, la, B=B, dk=dk, dv=dv, **dims)
            hs = _out_residual(hs, ml, mc, gla_w_o[j].astype(ACT),
                               gla=(o_f, o_b, qkvg, gla_norm_g[j]), **dims)
        elif kind == 1:
            glu = _project(hs, norm_mix_g[i], ml, mc, conv_w1[j].astype(ACT), mode="glu",
                           bias=conv_b1[j], **dims)
            hs = _conv_out(hs, ml, mc, glu, conv_dw[j], conv_dwb[j], conv_ln_g[j], conv_ln_b[j],
                           conv_w2[j].astype(ACT), conv_b2[j], **dims)
        else:
            lambda_init = 0.8 - 0.6 * math.exp(-0.3 * i)
            n_qk = diff_w_qkv.shape[2] // 3
            qkv = _project(hs, norm_mix_g[i], ml, mc, diff_w_qkv[j].astype(ACT), mode="rope",
                           rope=rope, n_rope_cols=2 * n_qk, n_q_cols=n_qk,
                           q_scale=DIFF_DK ** -0.5, **dims)
            att = _diff_attention(qkv, diff_lq1[j], diff_lk1[j], diff_lq2[j], diff_lk2[j],
                                  diff_subln_g[j], B=B, lambda_init=lambda_init, **dims)
            hs = _out_residual(hs, ml, mc, diff_w_o[j].astype(ACT), a=att, **dims)
        hs = _mlp(hs, norm_mlp_g[i], ml, mc, mlp_w1[i].astype(ACT), mlp_w2[i].astype(ACT), **dims)
    return _final_norm(hs, final_g, B=B, L=L, n_ctx=n_ctx)
```

```python
import functools
import math

import jax
import jax.numpy as jnp
from jax import lax
from jax.experimental import pallas as pl
from jax.experimental.pallas import tpu as pltpu

F32 = jnp.float32
ACT = jnp.bfloat16
EPS = 1e-6
N_MOD = 6
GLA_HEADS = 4
GLA_RANK = 16
GLA_GATE_NORM = 16.0
GLA_CHUNK = 64
CONV_WIDTH = 31
CONV_HALO = 16
DIFF_DK = 128
DIFF_DV = 256
GRID_W = 64
ROPE_BASE = 10000.0
LANE = 128
SUBLANE = 8
MOD_ROWS = 8
VMEM_LIMIT = 56 * 2 ** 20
ATTN_CHUNK = 128

def _params(n_axes):
    return pltpu.CompilerParams(dimension_semantics=("arbitrary",) * n_axes,
                                vmem_limit_bytes=VMEM_LIMIT)


def _dot(a, b):
    return jnp.dot(a, b, preferred_element_type=F32)


def _dot_nt(a, b):
    return lax.dot_general(a, b, (((1,), (1,)), ((), ())), preferred_element_type=F32)


def _dot_tn(a, b):
    return lax.dot_general(a, b, (((0,), (0,)), ((), ())), preferred_element_type=F32)


def _sigmoid(x):
    return 1.0 / (1.0 + jnp.exp(-x))


def _rms(x):
    return x * lax.rsqrt(jnp.mean(x * x, axis=-1, keepdims=True) + EPS)


def _row_tile(L, n_ctx, cap):
    best = n_ctx
    for m in range(1, L // n_ctx + 1):
        t = m * n_ctx
        if t <= cap and L % t == 0:
            best = t
    return best


def _col_tile(n, cap):
    best = LANE
    t = LANE
    while t <= min(n, cap):
        if n % t == 0:
            best = t
        t += LANE
    return best


def _mod_pick(ml_ref, mc_ref, k, D, ctx_row):
    lat = ml_ref[:, k * D:(k + 1) * D]
    ctx = mc_ref[:, k * D:(k + 1) * D]
    return jnp.where(ctx_row, ctx, lat)


def _ctx_rows(tm, n_ctx, first_tile):
    rows = lax.broadcasted_iota(jnp.int32, (tm, 1), 0)
    return jnp.logical_and(first_tile, rows < n_ctx)


def _norm_mod(h, g_ref, ml_ref, mc_ref, k_shift, k_scale, ctx_row):
    D = h.shape[-1]
    y = _rms(h) * g_ref[...]
    shift = _mod_pick(ml_ref, mc_ref, k_shift, D, ctx_row)
    scale = _mod_pick(ml_ref, mc_ref, k_scale, D, ctx_row)
    return y * (1.0 + scale) + shift


def _mod_kernel(c_ref, w_ref, b_ref, o_ref):
    c = c_ref[...]
    sc = (c * _sigmoid(c)).astype(ACT)
    o_ref[...] = _dot(sc, w_ref[...].astype(ACT)) + b_ref[...]


def _mod_vectors(cvec, mod_w, mod_b):
    depth, D, N = mod_w.shape
    tn = _col_tile(N, 1024)
    return pl.pallas_call(
        _mod_kernel,
        grid=(depth, N // tn),
        in_specs=[pl.BlockSpec((MOD_ROWS, D), lambda l, j: (0, 0)),
                  pl.BlockSpec((None, D, tn), lambda l, j: (l, 0, j)),
                  pl.BlockSpec((None, 1, tn), lambda l, j: (l, 0, j))],
        out_specs=pl.BlockSpec((None, MOD_ROWS, tn), lambda l, j: (l, 0, j)),
        out_shape=jax.ShapeDtypeStruct((depth, MOD_ROWS, N), F32),
        compiler_params=_params(2),
        name="mod_vectors",
    )(cvec, mod_w, mod_b.reshape(depth, 1, N))


def _rope_tile(acc, cos, sin_lo, sin_hi):
    up = pltpu.roll(acc, LANE - DIFF_DK // 4, axis=1)
    down = pltpu.roll(acc, DIFF_DK // 4, axis=1)
    return acc * cos + up * sin_lo + down * sin_hi


def _proj_kernel(*refs, mode, tpb, n_ctx, n_rope_tiles, q_tiles, q_scale):
    if mode == "glu":
        h_ref, g_ref, ml_ref, mc_ref, wa_ref, wg_ref, ba_ref, bg_ref, o_ref, u_scr = refs
    elif mode == "rope":
        h_ref, g_ref, ml_ref, mc_ref, w_ref, cos_ref, slo_ref, shi_ref, o_ref, u_scr = refs
    else:
        h_ref, g_ref, ml_ref, mc_ref, w_ref, o_ref, u_scr = refs
    i = pl.program_id(0)
    j = pl.program_id(1)
    tm = h_ref.shape[0]

    @pl.when(j == 0)
    def _():
        ctx_row = _ctx_rows(tm, n_ctx, i % tpb == 0)
        u_scr[...] = _norm_mod(h_ref[...], g_ref, ml_ref, mc_ref, 0, 1, ctx_row).astype(ACT)

    u = u_scr[...]
    if mode == "glu":
        a = _dot(u, wa_ref[...]) + ba_ref[...]
        g = _dot(u, wg_ref[...]) + bg_ref[...]
        o_ref[...] = (a * _sigmoid(g)).astype(o_ref.dtype)
    elif mode == "rope":
        acc = _dot(u, w_ref[...])

        @pl.when(j < n_rope_tiles)
        def _():
            scale = jnp.where(j < q_tiles, q_scale, 1.0)
            cos, slo, shi = cos_ref[...], slo_ref[...], shi_ref[...]
            for s in range(acc.shape[1] // LANE):
                seg = acc[:, s * LANE:(s + 1) * LANE]
                o_ref[:, s * LANE:(s + 1) * LANE] = (
                    _rope_tile(seg, cos, slo, shi) * scale).astype(o_ref.dtype)

        @pl.when(j >= n_rope_tiles)
        def _():
            o_ref[...] = acc.astype(o_ref.dtype)
    else:
        o_ref[...] = _dot(u, w_ref[...]).astype(o_ref.dtype)


def _project(hs, g, ml, mc, w, *, L, n_ctx, mode="plain", bias=None, rope=None,
             n_rope_cols=0, n_q_cols=0, q_scale=1.0):
    M, D = hs.shape
    N = w.shape[1]
    n_out = N // 2 if mode == "glu" else N
    tm = _row_tile(L, n_ctx, 768)
    tpb = L // tm
    tn = _col_tile(n_out, 1024)
    if mode == "rope":
        while n_rope_cols % tn or n_q_cols % tn:
            tn -= LANE
    nj = n_out // tn
    row = lambda i, j: (i, 0)
    fixed = lambda i, j: (0, 0)
    in_specs = [pl.BlockSpec((tm, D), row),
                pl.BlockSpec((1, D), fixed),
                pl.BlockSpec((None, 1, N_MOD * D), lambda i, j: (i // tpb, 0, 0)),
                pl.BlockSpec((1, N_MOD * D), fixed)]
    args = [hs, g.reshape(1, D), ml, mc]
    if mode == "glu":
        in_specs += [pl.BlockSpec((D, tn), lambda i, j: (0, j)),
                     pl.BlockSpec((D, tn), lambda i, j: (0, j + nj)),
                     pl.BlockSpec((1, tn), lambda i, j: (0, j)),
                     pl.BlockSpec((1, tn), lambda i, j: (0, j + nj))]
        b2 = bias.reshape(1, N)
        args += [w, w, b2, b2]
    else:
        in_specs.append(pl.BlockSpec((D, tn), lambda i, j: (0, j)))
        args.append(w)
        if mode == "rope":
            tab = pl.BlockSpec((tm, LANE), lambda i, j: (i % tpb, 0))
            in_specs += [tab, tab, tab]
            args += list(rope)
    kern = functools.partial(_proj_kernel, mode=mode, tpb=tpb, n_ctx=n_ctx,
                             n_rope_tiles=n_rope_cols // tn, q_tiles=n_q_cols // tn,
                             q_scale=q_scale)
    return pl.pallas_call(
        kern,
        grid=(M // tm, nj),
        in_specs=in_specs,
        out_specs=pl.BlockSpec((tm, tn), lambda i, j: (i, j)),
        out_shape=jax.ShapeDtypeStruct((M, n_out), ACT),
        scratch_shapes=[pltpu.VMEM((tm, D), ACT)],
        compiler_params=_params(2),
        name="project_" + mode,
    )(*args)


def _mlp_kernel(h_ref, g_ref, ml_ref, mc_ref, w1_ref, w2_ref, o_ref, u_scr, acc_scr,
                *, tpb, n_ctx):
    i = pl.program_id(0)
    j = pl.program_id(1)
    tm, D = h_ref.shape
    ctx_row = _ctx_rows(tm, n_ctx, i % tpb == 0)

    @pl.when(j == 0)
    def _():
        u_scr[...] = _norm_mod(h_ref[...], g_ref, ml_ref, mc_ref, 3, 4, ctx_row).astype(ACT)
        acc_scr[...] = jnp.zeros_like(acc_scr)

    hid = jnp.maximum(_dot(u_scr[...], w1_ref[...]), 0.0)
    acc_scr[...] += _dot((hid * hid).astype(ACT), w2_ref[...])

    @pl.when(j == pl.num_programs(1) - 1)
    def _():
        gate = _mod_pick(ml_ref, mc_ref, 5, D, ctx_row)
        o_ref[...] = h_ref[...] + gate * acc_scr[...]


def _mlp(hs, g, ml, mc, w1, w2, *, L, n_ctx):
    M, D = hs.shape
    Hd = w1.shape[1]
    tm = _row_tile(L, n_ctx, 768)
    tpb = L // tm
    th = _col_tile(Hd, 512)
    row = lambda i, j: (i, 0)
    fixed = lambda i, j: (0, 0)
    return pl.pallas_call(
        functools.partial(_mlp_kernel, tpb=tpb, n_ctx=n_ctx),
        grid=(M // tm, Hd // th),
        in_specs=[pl.BlockSpec((tm, D), row),
                  pl.BlockSpec((1, D), fixed),
                  pl.BlockSpec((None, 1, N_MOD * D), lambda i, j: (i // tpb, 0, 0)),
                  pl.BlockSpec((1, N_MOD * D), fixed),
                  pl.BlockSpec((D, th), lambda i, j: (0, j)),
                  pl.BlockSpec((th, D), lambda i, j: (j, 0))],
        out_specs=pl.BlockSpec((tm, D), row),
        out_shape=jax.ShapeDtypeStruct((M, D), F32),
        scratch_shapes=[pltpu.VMEM((tm, D), ACT), pltpu.VMEM((tm, D), F32)],
        compiler_params=_params(2),
        name="mlp",
    )(hs, g.reshape(1, D), ml, mc, w1, w2)


def _out_kernel(*refs, mode, tpb, n_ctx, n_heads):
    if mode == "gla":
        h_ref, ml_ref, mc_ref, w_ref, of_ref, ob_ref, gt_ref, ng_ref, o_ref = refs
    else:
        h_ref, ml_ref, mc_ref, w_ref, a_ref, o_ref = refs
    i = pl.program_id(0)
    tm, D = h_ref.shape
    ctx_row = _ctx_rows(tm, n_ctx, i % tpb == 0)
    if mode == "gla":
        o = of_ref[...].astype(F32) + ob_ref[...].astype(F32)
        dv = o.shape[1] // n_heads
        ng = ng_ref[...]
        o = jnp.concatenate([_rms(o[:, hh * dv:(hh + 1) * dv]) * ng for hh in range(n_heads)],
                            axis=1)
        gt = gt_ref[...].astype(F32)
        a = (o * (gt * _sigmoid(gt))).astype(ACT)
    else:
        a = a_ref[...]
    gate = _mod_pick(ml_ref, mc_ref, 2, D, ctx_row)
    o_ref[...] = h_ref[...] + gate * _dot(a, w_ref[...])


def _out_residual(hs, ml, mc, w, *, L, n_ctx, a=None, gla=None):
    M, D = hs.shape
    K = w.shape[0]
    tm = _row_tile(L, n_ctx, 256)
    tpb = L // tm
    row = lambda i: (i, 0)
    fixed = lambda i: (0, 0)
    in_specs = [pl.BlockSpec((tm, D), row),
                pl.BlockSpec((None, 1, N_MOD * D), lambda i: (i // tpb, 0, 0)),
                pl.BlockSpec((1, N_MOD * D), fixed),
                pl.BlockSpec((K, D), fixed)]
    args = [hs, ml, mc, w]
    if gla is not None:
        o_f, o_b, qkvg, norm_g = gla
        g_blk = qkvg.shape[1] // K - 1
        in_specs += [pl.BlockSpec((tm, K), row), pl.BlockSpec((tm, K), row),
                     pl.BlockSpec((tm, K), lambda i: (i, g_blk)),
                     pl.BlockSpec((1, norm_g.shape[0]), fixed)]
        args += [o_f, o_b, qkvg, norm_g.reshape(1, -1)]
        mode = "gla"
    else:
        in_specs.append(pl.BlockSpec((tm, K), row))
        args.append(a)
        mode = "plain"
    return pl.pallas_call(
        functools.partial(_out_kernel, mode=mode, tpb=tpb, n_ctx=n_ctx, n_heads=GLA_HEADS),
        grid=(M // tm,),
        in_specs=in_specs,
        out_specs=pl.BlockSpec((tm, D), row),
        out_shape=jax.ShapeDtypeStruct((M, D), F32),
        compiler_params=_params(1),
        name="out_" + mode,
    )(*args)


def _gate_kernel(h_ref, g_ref, ml_ref, mc_ref, wa1_ref, wa2_ref, ba_ref, o_ref, *, tpb, n_ctx):
    i = pl.program_id(0)
    tm = h_ref.shape[0]
    ctx_row = _ctx_rows(tm, n_ctx, i % tpb == 0)
    u = _norm_mod(h_ref[...], g_ref, ml_ref, mc_ref, 0, 1, ctx_row).astype(ACT)
    t = _dot(u, wa1_ref[...]).astype(ACT)
    z = _dot(t, wa2_ref[...]) + ba_ref[...]
    log_sig = jnp.minimum(z, 0.0) - jnp.log(1.0 + jnp.exp(-jnp.abs(z)))
    o_ref[...] = log_sig / GLA_GATE_NORM


def _gla_gates(hs, g, ml, mc, wa1, wa2, ba, *, L, n_ctx):
    M, D = hs.shape
    R, N = wa2.shape
    tm = _row_tile(L, n_ctx, 768)
    tpb = L // tm
    row = lambda i: (i, 0)
    fixed = lambda i: (0, 0)
    return pl.pallas_call(
        functools.partial(_gate_kernel, tpb=tpb, n_ctx=n_ctx),
        grid=(M // tm,),
        in_specs=[pl.BlockSpec((tm, D), row),
                  pl.BlockSpec((1, D), fixed),
                  pl.BlockSpec((None, 1, N_MOD * D), lambda i: (i // tpb, 0, 0)),
                  pl.BlockSpec((1, N_MOD * D), fixed),
                  pl.BlockSpec((D, R), fixed),
                  pl.BlockSpec((R, N), fixed),
                  pl.BlockSpec((1, N), fixed)],
        out_specs=pl.BlockSpec((tm, N), row),
        out_shape=jax.ShapeDtypeStruct((M, N), F32),
        compiler_params=_params(1),
        name="gla_gates",
    )(hs, g.reshape(1, D), ml, mc, wa1, wa2, ba.reshape(1, N))


def _gla_chunk(q_ref, k_ref, v_ref, g_ref, o_ref, s_scr, c0, mask, last_row, q_scale):
    C = GLA_CHUNK
    rows = pl.ds(c0, C)
    g = g_ref[rows, :]
    tri = mask.astype(ACT)
    g_hi = g.astype(ACT)
    rest = g - g_hi.astype(F32)
    g_mid = rest.astype(ACT)
    g_lo = (rest - g_mid.astype(F32)).astype(ACT)
    b = _dot(tri, g_hi) + _dot(tri, g_mid) + _dot(tri, g_lo)
    b_last = b[last_row:last_row + 1, :]
    q = q_ref[rows, :].astype(F32) * q_scale
    k = k_ref[rows, :].astype(F32)
    q_dec = (q * jnp.exp(b)).astype(ACT)
    k_inv = (k * jnp.exp(-b)).astype(ACT)
    k_end = (k * jnp.exp(b_last - b)).astype(ACT)
    att = jnp.where(mask, _dot_nt(q_dec, k_inv), 0.0).astype(ACT)
    v = v_ref[rows, :]
    state = s_scr[...]
    o = _dot(att, v) + _dot_nt(q_dec, state.astype(ACT))
    o_ref[rows, :] = o.astype(o_ref.dtype)
    s_scr[...] = state * jnp.exp(b_last) + _dot_tn(v, k_end)


def _gla_kernel(qf_ref, kf_ref, vf_ref, gf_ref, qb_ref, kb_ref, vb_ref, gb_ref,
                of_ref, ob_ref, sf_scr, sb_scr, *, q_scale):
    @pl.when(pl.program_id(2) == 0)
    def _():
        sf_scr[...] = jnp.zeros_like(sf_scr)
        sb_scr[...] = jnp.zeros_like(sb_scr)

    C = GLA_CHUNK
    n_chunks = qf_ref.shape[0] // C
    r = lax.broadcasted_iota(jnp.int32, (C, C), 0)
    c = lax.broadcasted_iota(jnp.int32, (C, C), 1)
    for n in range(n_chunks):
        _gla_chunk(qf_ref, kf_ref, vf_ref, gf_ref, of_ref, sf_scr, n * C, r >= c, C - 1, q_scale)
        m = n_chunks - 1 - n
        _gla_chunk(qb_ref, kb_ref, vb_ref, gb_ref, ob_ref, sb_scr, m * C, r <= c, 0, q_scale)


def _gla_scan(qkvg, la, *, B, L, n_ctx, dk, dv):
    M = qkvg.shape[0]
    H = GLA_HEADS
    blk = n_ctx
    nb = L // blk
    kq = (H * dk) // dk
    kv = (2 * H * dk) // dv

    def fwd(n):
        return n

    def bwd(n):
        return jnp.where(n == 0, 0, nb - n)

    def spec(width, off, order):
        return pl.BlockSpec((blk, width), lambda b, h, n: (b * nb + order(n), off + h))

    out_spec = lambda order: pl.BlockSpec((blk, dv), lambda b, h, n: (b * nb + order(n), h))
    return pl.pallas_call(
        functools.partial(_gla_kernel, q_scale=dk ** -0.5),
        grid=(B, H, nb),
        in_specs=[spec(dk, 0, fwd), spec(dk, kq, fwd), spec(dv, kv, fwd), spec(dk, 0, fwd),
                  spec(dk, 0, bwd), spec(dk, kq, bwd), spec(dv, kv, bwd), spec(dk, kq, bwd)],
        out_specs=[out_spec(fwd), out_spec(bwd)],
        out_shape=[jax.ShapeDtypeStruct((M, H * dv), ACT)] * 2,
        scratch_shapes=[pltpu.VMEM((dv, dk), F32), pltpu.VMEM((dv, dk), F32)],
        compiler_params=_params(3),
        name="gla_scan",
    )(qkvg, qkvg, qkvg, la, qkvg, qkvg, qkvg, la)


def _conv_kernel(h_ref, ml_ref, mc_ref, x_ref, prev_ref, next_ref, dw_ref, dwb_ref,
                 lng_ref, lnb_ref, w_ref, b_ref, o_ref, xw_scr, y_scr, *, tpb):
    i = pl.program_id(0)
    tm, D = h_ref.shape
    t = i % tpb
    has_prev = t > 1
    has_next = jnp.logical_and(t > 0, t < tpb - 1)
    xw_scr[0, 0:CONV_HALO, :] = jnp.where(has_prev, prev_ref[...].astype(F32), 0.0)
    xw_scr[0, CONV_HALO:CONV_HALO + tm, :] = x_ref[...].astype(F32)
    xw_scr[0, CONV_HALO + tm:, :] = jnp.where(has_next, next_ref[...].astype(F32), 0.0)
    n_shift = tm + 2 * CONV_HALO - SUBLANE
    for r in range(1, SUBLANE):
        xw_scr[r, 0:n_shift, :] = xw_scr[0, r:r + n_shift, :]
    first = CONV_HALO - CONV_WIDTH // 2
    rb = min(tm, 128)

    def col_body(cb, carry):
        cols = pl.ds(pl.multiple_of(cb * LANE, LANE), LANE)
        for r0 in range(0, tm, rb):
            acc = jnp.zeros((rb, LANE), F32)
            for k in range(CONV_WIDTH):
                a, r = divmod(first + k, SUBLANE)
                tap = xw_scr[r, pl.ds(r0 + a * SUBLANE, rb), cols]
                acc = acc + tap * dw_ref[pl.ds(k, 1), cols]
            y_scr[pl.ds(r0, rb), cols] = acc + dwb_ref[:, cols]
        return carry

    lax.fori_loop(0, D // LANE, col_body, 0)
    y = y_scr[...]
    yc = y - jnp.mean(y, axis=-1, keepdims=True)
    yn = yc * lax.rsqrt(jnp.mean(yc * yc, axis=-1, keepdims=True) + EPS)
    yn = yn * lng_ref[...] + lnb_ref[...]
    a = (yn * _sigmoid(yn)).astype(ACT)
    gate = _mod_pick(ml_ref, mc_ref, 2, D, t == 0)
    o_ref[...] = h_ref[...] + gate * (_dot(a, w_ref[...]) + b_ref[...])


def _conv_out(hs, ml, mc, glu, dw, dwb, ln_g, ln_b, w2, b2, *, L, n_ctx):
    M, D = hs.shape
    tm = n_ctx
    tpb = L // tm
    hb = tm // CONV_HALO
    n_halo = M // CONV_HALO
    row = lambda i: (i, 0)
    fixed = lambda i: (0, 0)
    vec = pl.BlockSpec((1, D), fixed)
    return pl.pallas_call(
        functools.partial(_conv_kernel, tpb=tpb),
        grid=(M // tm,),
        in_specs=[pl.BlockSpec((tm, D), row),
                  pl.BlockSpec((None, 1, N_MOD * D), lambda i: (i // tpb, 0, 0)),
                  pl.BlockSpec((1, N_MOD * D), fixed),
                  pl.BlockSpec((tm, D), row),
                  pl.BlockSpec((CONV_HALO, D), lambda i: (jnp.maximum(i * hb - 1, 0), 0)),
                  pl.BlockSpec((CONV_HALO, D), lambda i: (jnp.minimum((i + 1) * hb, n_halo - 1), 0)),
                  pl.BlockSpec((CONV_WIDTH, D), fixed),
                  vec, vec, vec,
                  pl.BlockSpec((D, D), fixed),
                  vec],
        out_specs=pl.BlockSpec((tm, D), row),
        out_shape=jax.ShapeDtypeStruct((M, D), F32),
        scratch_shapes=[pltpu.VMEM((SUBLANE, tm + 2 * CONV_HALO, D), F32),
                        pltpu.VMEM((tm, D), F32)],
        compiler_params=_params(1),
        name="conv_out",
    )(hs, ml, mc, glu, glu, glu, dw, dwb.reshape(1, D), ln_g.reshape(1, D), ln_b.reshape(1, D),
      w2, b2.reshape(1, D))


def _attn_kernel(q_ref, k_ref, v_ref, lq1_ref, lk1_ref, lq2_ref, lk2_ref, sg_ref, o_ref,
                 vtc_scr, vtl_scr, acc_scr, m_scr, l_scr, s0_scr, s1_scr, p0_scr, p1_scr,
                 a0_scr, a1_scr, x0_scr, x1_scr, *, n_ctx, tk, lambda_init):
    qi = pl.program_id(2)
    n_lat_steps = vtl_scr.shape[0]

    @pl.when(qi == 0)
    def _():
        vtc_scr[...] = v_ref[0:n_ctx, :].astype(F32).T.astype(ACT)

        def fill(t, carry):
            rows = pl.ds(pl.multiple_of(n_ctx + t * tk, n_ctx), tk)
            vtl_scr[t] = v_ref[rows, :].astype(F32).T.astype(ACT)
            return carry
        lax.fori_loop(0, n_lat_steps, fill, 0)

    acc_scr[...] = jnp.zeros_like(acc_scr)
    m_scr[...] = jnp.full_like(m_scr, -1e30)
    l_scr[...] = jnp.zeros_like(l_scr)

    def step(k0, size, vt):
        for j in range(2):
            lanes = slice(j * DIFF_DK, (j + 1) * DIFF_DK)
            st = _dot_nt(k_ref[pl.ds(k0, size), lanes], q_ref[:, lanes])
            m_prev = m_scr[j]
            m_new = jnp.maximum(m_prev, jnp.max(st, axis=0, keepdims=True))
            alpha = jnp.exp2(m_prev - m_new)
            p = jnp.exp2(st - m_new)
            l_scr[j] = alpha * l_scr[j] + jnp.sum(p, axis=0, keepdims=True)
            acc_scr[j] = alpha * acc_scr[j] + _dot(vt, p.astype(ACT))
            m_scr[j] = m_new

    step(0, n_ctx, vtc_scr[...])

    def scores(t, s_scr, x_scr):
        for j in range(2):
            lanes = slice(j * DIFF_DK, (j + 1) * DIFF_DK)
            part = None
            for c0 in range(0, tk, ATTN_CHUNK):
                rows = pl.ds(pl.multiple_of(n_ctx + t * tk + c0, ATTN_CHUNK), ATTN_CHUNK)
                st = _dot_nt(k_ref[rows, lanes], q_ref[:, lanes])
                s_scr[j, c0:c0 + ATTN_CHUNK, :] = st
                part = st if part is None else jnp.maximum(part, st)
            x_scr[j] = jnp.max(part, axis=0, keepdims=True)

    def softmax(s_scr, x_scr, p_scr, a_scr):
        for j in range(2):
            m_prev = m_scr[j]
            m_new = jnp.maximum(m_prev, x_scr[j])
            alpha = jnp.exp2(m_prev - m_new)
            part = None
            for c0 in range(0, tk, ATTN_CHUNK):
                p = jnp.exp2(s_scr[j, c0:c0 + ATTN_CHUNK, :] - m_new)
                p_scr[j, c0:c0 + ATTN_CHUNK, :] = p.astype(ACT)
                part = p if part is None else part + p
            l_scr[j] = alpha * l_scr[j] + jnp.sum(part, axis=0, keepdims=True)
            m_scr[j] = m_new
            a_scr[j] = alpha

    def values(t, p_scr, a_scr):
        vt = vtl_scr[t]
        for j in range(2):
            acc_scr[j] = a_scr[j] * acc_scr[j] + _dot(vt, p_scr[j])

    @pl.when(qi > 0)
    def _():
        scores(0, s0_scr, x0_scr)
        scores(1, s1_scr, x1_scr)
        softmax(s0_scr, x0_scr, p0_scr, a0_scr)

        def body(i, carry):
            t = 2 * i
            scores(t + 2, s0_scr, x0_scr)
            softmax(s1_scr, x1_scr, p1_scr, a1_scr)
            values(t, p0_scr, a0_scr)
            scores(t + 3, s1_scr, x1_scr)
            softmax(s0_scr, x0_scr, p0_scr, a0_scr)
            values(t + 1, p1_scr, a1_scr)
            return carry
        lax.fori_loop(0, n_lat_steps // 2 - 1, body, 0, unroll=True)
        softmax(s1_scr, x1_scr, p1_scr, a1_scr)
        values(n_lat_steps - 2, p0_scr, a0_scr)
        values(n_lat_steps - 1, p1_scr, a1_scr)

    lam = (jnp.exp(jnp.sum(lq1_ref[...] * lk1_ref[...], axis=-1, keepdims=True))
           - jnp.exp(jnp.sum(lq2_ref[...] * lk2_ref[...], axis=-1, keepdims=True)) + lambda_init)
    ot = acc_scr[0] * (1.0 / l_scr[0]) - lam * (acc_scr[1] * (1.0 / l_scr[1]))
    ot = ot * lax.rsqrt(jnp.mean(ot * ot, axis=0, keepdims=True) + EPS)
    ot = ot * (sg_ref[...] * (1.0 - lambda_init))
    o_ref[...] = ot.T.astype(o_ref.dtype)


def _diff_attention(qkv, lq1, lk1, lq2, lk2, subln_g, *, B, L, n_ctx, lambda_init):
    M, N = qkv.shape
    width = 2 * DIFF_DK
    H = N // (3 * width)
    tq = n_ctx
    nq = L // tq
    tk = _row_tile(L - n_ctx, n_ctx, 512)
    n_steps = (L - n_ctx) // tk
    assert n_steps >= 2 and n_steps % 2 == 0
    stage = lambda dtype: pltpu.VMEM((2, tk, tq), dtype)
    stat = pltpu.VMEM((2, 1, tq), F32)
    fixed = lambda b, h, i: (0, 0)
    vec = pl.BlockSpec((1, DIFF_DK), fixed)
    return pl.pallas_call(
        functools.partial(_attn_kernel, n_ctx=n_ctx, tk=tk, lambda_init=lambda_init),
        grid=(B, H, nq),
        in_specs=[pl.BlockSpec((tq, width), lambda b, h, i: (b * nq + i, h)),
                  pl.BlockSpec((L, width), lambda b, h, i: (b, H + h)),
                  pl.BlockSpec((L, DIFF_DV), lambda b, h, i: (b, 2 * H + h)),
                  vec, vec, vec, vec,
                  pl.BlockSpec((DIFF_DV, 1), fixed)],
        out_specs=pl.BlockSpec((tq, DIFF_DV), lambda b, h, i: (b * nq + i, h)),
        out_shape=jax.ShapeDtypeStruct((M, H * DIFF_DV), ACT),
        scratch_shapes=[pltpu.VMEM((DIFF_DV, n_ctx), ACT),
                        pltpu.VMEM((n_steps, DIFF_DV, tk), ACT),
                        pltpu.VMEM((2, DIFF_DV, tq), F32), stat, stat,
                        stage(F32), stage(F32), stage(ACT), stage(ACT), stat, stat, stat, stat],
        compiler_params=_params(3),
        name="diff_attention",
    )(qkv, qkv, qkv, lq1.reshape(1, -1), lk1.reshape(1, -1), lq2.reshape(1, -1),
      lk2.reshape(1, -1), subln_g.reshape(-1, 1))


def _rope_tables(n_ctx, n_lat):
    axis_dim = DIFF_DK // 2
    rows = n_lat // GRID_W
    pos_row = jnp.repeat(jnp.arange(rows, dtype=F32), GRID_W)
    pos_col = jnp.tile(jnp.arange(GRID_W, dtype=F32), rows)
    inv_freq = ROPE_BASE ** (-jnp.arange(0, axis_dim, 2, dtype=F32) / axis_dim)
    ang_r = pos_row[:, None] * inv_freq
    ang_c = pos_col[:, None] * inv_freq
    ang = jnp.concatenate([ang_r, ang_r, ang_c, ang_c], axis=-1)
    cos = jnp.concatenate([jnp.ones((n_ctx, DIFF_DK), F32), jnp.cos(ang)], axis=0)
    sin = jnp.concatenate([jnp.zeros((n_ctx, DIFF_DK), F32), jnp.sin(ang)], axis=0)
    first_half = (jnp.arange(DIFF_DK) % (axis_dim)) < axis_dim // 2
    return cos, jnp.where(first_half, -sin, 0.0), jnp.where(first_half, 0.0, sin)


def _final_kernel(h_ref, g_ref, o_ref):
    o_ref[...] = _rms(h_ref[...]) * g_ref[...]


def _final_norm(hs, g, *, B, L, n_ctx):
    D = hs.shape[1]
    tm = n_ctx
    skip = n_ctx // tm
    n_lat = L - n_ctx
    return pl.pallas_call(
        _final_kernel,
        grid=(B, n_lat // tm),
        in_specs=[pl.BlockSpec((None, tm, D), lambda b, i: (b, i + skip, 0)),
                  pl.BlockSpec((1, D), lambda b, i: (0, 0))],
        out_specs=pl.BlockSpec((None, tm, D), lambda b, i: (b, i, 0)),
        out_shape=jax.ShapeDtypeStruct((B, n_lat, D), F32),
        compiler_params=_params(2),
        name="final_norm",
    )(hs.reshape(B, L, D), g.reshape(1, D))


def kernel(x, c, ctx, c_ctx, mod_w, mod_b, norm_mix_g, norm_mlp_g, mlp_w1, mlp_w2, gla_w_in, gla_wa1_f, gla_wa2_f, gla_ba_f, gla_wa1_b, gla_wa2_b, gla_ba_b, gla_norm_g, gla_w_o, conv_w1, conv_b1, conv_dw, conv_dwb, conv_ln_g, conv_ln_b, conv_w2, conv_b2, diff_w_qkv, diff_lq1, diff_lk1, diff_lq2, diff_lk2, diff_subln_g, diff_w_o, final_g):
    B, n_lat, D = x.shape
    n_ctx = ctx.shape[1]
    L = n_ctx + n_lat
    depth = mod_w.shape[0]
    assert B + 1 <= MOD_ROWS and n_lat % n_ctx == 0 and n_ctx % GLA_CHUNK == 0
    dims = dict(L=L, n_ctx=n_ctx)

    hs = jnp.concatenate([ctx, x], axis=1).reshape(B * L, D)
    cvec = jnp.zeros((MOD_ROWS, D), F32).at[:B].set(c).at[B].set(c_ctx)
    mods = _mod_vectors(cvec, mod_w, mod_b)
    rope = _rope_tables(n_ctx, n_lat)

    for i in range(depth):
        ml = mods[i, :B][:, None, :]
        mc = mods[i, B:B + 1]
        kind, j = i % 3, i // 3
        if kind == 0:
            hk = gla_wa2_f.shape[2]
            dk, dv = hk // GLA_HEADS, gla_w_o.shape[1] // GLA_HEADS
            qkvg = _project(hs, norm_mix_g[i], ml, mc, gla_w_in[j].astype(ACT), **dims)
            wa1 = jnp.zeros((D, LANE), F32).at[:, :GLA_RANK].set(gla_wa1_f[j])
            wa1 = wa1.at[:, GLA_RANK:2 * GLA_RANK].set(gla_wa1_b[j])
            wa2 = jnp.zeros((LANE, 2 * hk), F32).at[:GLA_RANK, :hk].set(gla_wa2_f[j])
            wa2 = wa2.at[GLA_RANK:2 * GLA_RANK, hk:].set(gla_wa2_b[j])
            ba = jnp.concatenate([gla_ba_f[j], gla_ba_b[j]])
            la = _gla_gates(hs, norm_mix_g[i], ml, mc, wa1.astype(ACT), wa2.astype(ACT), ba, **dims)
            o_f, o_b = _gla_scan(qkvg, la, B=B, dk=dk, dv=dv, **dims)
            hs = _out_residual(hs, ml, mc, gla_w_o[j].astype(ACT),
                               gla=(o_f, o_b, qkvg, gla_norm_g[j]), **dims)
        elif kind == 1:
            glu = _project(hs, norm_mix_g[i], ml, mc, conv_w1[j].astype(ACT), mode="glu",
                           bias=conv_b1[j], **dims)
            hs = _conv_out(hs, ml, mc, glu, conv_dw[j], conv_dwb[j], conv_ln_g[j], conv_ln_b[j],
                           conv_w2[j].astype(ACT), conv_b2[j], **dims)
        else:
            lambda_init = 0.8 - 0.6 * math.exp(-0.3 * i)
            n_qk = diff_w_qkv.shape[2] // 3
            qkv = _project(hs, norm_mix_g[i], ml, mc, diff_w_qkv[j].astype(ACT), mode="rope",
                           rope=rope, n_rope_cols=2 * n_qk, n_q_cols=n_qk,
                           q_scale=DIFF_DK ** -0.5 * math.log2(math.e), **dims)
            att = _diff_attention(qkv, diff_lq1[j], diff_lk1[j], diff_lq2[j], diff_lk2[j],
                                  diff_subln_g[j], B=B, lambda_init=lambda_init, **dims)
            hs = _out_residual(hs, ml, mc, diff_w_o[j].astype(ACT), a=att, **dims)
        hs = _mlp(hs, norm_mlp_g[i], ml, mc, mlp_w1[i].astype(ACT), mlp_w2[i].astype(ACT), **dims)
    return _final_norm(hs, final_g, B=B, L=L, n_ctx=n_ctx)
```

```python
import functools
import math

import jax
import jax.numpy as jnp
from jax import lax
from jax.experimental import pallas as pl
from jax.experimental.pallas import tpu as pltpu

F32 = jnp.float32
ACT = jnp.bfloat16
EPS = 1e-6
N_MOD = 6
GLA_HEADS = 4
GLA_RANK = 16
GLA_GATE_NORM = 16.0
GLA_CHUNK = 64
CONV_WIDTH = 31
CONV_HALO = 16
DIFF_DK = 128
DIFF_DV = 256
GRID_W = 64
ROPE_BASE = 10000.0
LANE = 128
SUBLANE = 8
MOD_ROWS = 8
VMEM_LIMIT = 56 * 2 ** 20
ATTN_CHUNK = 128

def _params(n_axes):
    return pltpu.CompilerParams(dimension_semantics=("arbitrary",) * n_axes,
                                vmem_limit_bytes=VMEM_LIMIT)


def _dot(a, b):
    return jnp.dot(a, b, preferred_element_type=F32)


def _dot_nt(a, b):
    return lax.dot_general(a, b, (((1,), (1,)), ((), ())), preferred_element_type=F32)


def _dot_tn(a, b):
    return lax.dot_general(a, b, (((0,), (0,)), ((), ())), preferred_element_type=F32)


def _sigmoid(x):
    return 1.0 / (1.0 + jnp.exp(-x))


def _rms(x):
    return x * lax.rsqrt(jnp.mean(x * x, axis=-1, keepdims=True) + EPS)


def _row_tile(L, n_ctx, cap):
    best = n_ctx
    for m in range(1, L // n_ctx + 1):
        t = m * n_ctx
        if t <= cap and L % t == 0:
            best = t
    return best


def _col_tile(n, cap):
    best = LANE
    t = LANE
    while t <= min(n, cap):
        if n % t == 0:
            best = t
        t += LANE
    return best


def _mod_pick(ml_ref, mc_ref, k, D, ctx_row):
    lat = ml_ref[:, k * D:(k + 1) * D]
    ctx = mc_ref[:, k * D:(k + 1) * D]
    return jnp.where(ctx_row, ctx, lat)


def _ctx_rows(tm, n_ctx, first_tile):
    rows = lax.broadcasted_iota(jnp.int32, (tm, 1), 0)
    return jnp.logical_and(first_tile, rows < n_ctx)


def _norm_mod_store(h_ref, g_ref, ml_ref, mc_ref, k_shift, k_scale, first_tile, n_ctx, store):
    h = h_ref[...]
    tm, D = h.shape
    y = _rms(h)
    g = g_ref[...]
    lat_a = g * (1.0 + ml_ref[:, k_scale * D:(k_scale + 1) * D])
    lat_b = ml_ref[:, k_shift * D:(k_shift + 1) * D]

    @pl.when(jnp.logical_not(first_tile))
    def _():
        store(y * lat_a + lat_b)

    @pl.when(first_tile)
    def _():
        ctx_row = lax.broadcasted_iota(jnp.int32, (tm, 1), 0) < n_ctx
        ctx_a = g * (1.0 + mc_ref[:, k_scale * D:(k_scale + 1) * D])
        ctx_b = mc_ref[:, k_shift * D:(k_shift + 1) * D]
        store(y * jnp.where(ctx_row, ctx_a, lat_a) + jnp.where(ctx_row, ctx_b, lat_b))


def _mod_kernel(c_ref, w_ref, b_ref, o_ref):
    c = c_ref[...]
    sc = (c * _sigmoid(c)).astype(ACT)
    o_ref[...] = _dot(sc, w_ref[...].astype(ACT)) + b_ref[...]


def _mod_vectors(cvec, mod_w, mod_b):
    depth, D, N = mod_w.shape
    tn = _col_tile(N, 1024)
    return pl.pallas_call(
        _mod_kernel,
        grid=(depth, N // tn),
        in_specs=[pl.BlockSpec((MOD_ROWS, D), lambda l, j: (0, 0)),
                  pl.BlockSpec((None, D, tn), lambda l, j: (l, 0, j)),
                  pl.BlockSpec((None, 1, tn), lambda l, j: (l, 0, j))],
        out_specs=pl.BlockSpec((None, MOD_ROWS, tn), lambda l, j: (l, 0, j)),
        out_shape=jax.ShapeDtypeStruct((depth, MOD_ROWS, N), F32),
        compiler_params=_params(2),
        name="mod_vectors",
    )(cvec, mod_w, mod_b.reshape(depth, 1, N))


def _rope_tile(acc, cos, sin_lo, sin_hi):
    up = pltpu.roll(acc, LANE - DIFF_DK // 4, axis=1)
    down = pltpu.roll(acc, DIFF_DK // 4, axis=1)
    return acc * cos + up * sin_lo + down * sin_hi


def _proj_kernel(*refs, mode, tpb, n_ctx, n_rope_tiles, q_tiles, q_scale):
    if mode == "glu":
        h_ref, g_ref, ml_ref, mc_ref, wa_ref, wg_ref, ba_ref, bg_ref, o_ref, u_scr = refs
    elif mode == "rope":
        h_ref, g_ref, ml_ref, mc_ref, w_ref, cos_ref, slo_ref, shi_ref, o_ref, u_scr = refs
    else:
        h_ref, g_ref, ml_ref, mc_ref, w_ref, wa1_ref, o_ref, t_ref, u_scr = refs
    i = pl.program_id(0)
    j = pl.program_id(1)

    @pl.when(j == 0)
    def _():
        def store(u):
            u_scr[...] = u.astype(ACT)
        _norm_mod_store(h_ref, g_ref, ml_ref, mc_ref, 0, 1, i % tpb == 0, n_ctx, store)
        if mode == "gla":
            t_ref[...] = _dot(u_scr[...], wa1_ref[...]).astype(t_ref.dtype)

    u = u_scr[...]
    if mode == "glu":
        a = _dot(u, wa_ref[...]) + ba_ref[...]
        g = _dot(u, wg_ref[...]) + bg_ref[...]
        o_ref[...] = (a * _sigmoid(g)).astype(o_ref.dtype)
    elif mode == "rope":
        acc = _dot(u, w_ref[...])

        @pl.when(j < n_rope_tiles)
        def _():
            scale = jnp.where(j < q_tiles, q_scale, 1.0)
            cos, slo, shi = cos_ref[...], slo_ref[...], shi_ref[...]
            for s in range(acc.shape[1] // LANE):
                seg = acc[:, s * LANE:(s + 1) * LANE]
                o_ref[:, s * LANE:(s + 1) * LANE] = (
                    _rope_tile(seg, cos, slo, shi) * scale).astype(o_ref.dtype)

        @pl.when(j >= n_rope_tiles)
        def _():
            o_ref[...] = acc.astype(o_ref.dtype)
    else:
        o_ref[...] = _dot(u, w_ref[...]).astype(o_ref.dtype)


def _project(hs, g, ml, mc, w, *, L, n_ctx, mode, bias=None, rope=None, wa1=None,
             n_rope_cols=0, n_q_cols=0, q_scale=1.0):
    M, D = hs.shape
    N = w.shape[1]
    n_out = N // 2 if mode == "glu" else N
    tm = _row_tile(L, n_ctx, 768)
    tpb = L // tm
    tn = _col_tile(n_out, 1024)
    if mode == "rope":
        while n_rope_cols % tn or n_q_cols % tn:
            tn -= LANE
    nj = n_out // tn
    row = lambda i, j: (i, 0)
    fixed = lambda i, j: (0, 0)
    in_specs = [pl.BlockSpec((tm, D), row),
                pl.BlockSpec((1, D), fixed),
                pl.BlockSpec((None, 1, N_MOD * D), lambda i, j: (i // tpb, 0, 0)),
                pl.BlockSpec((1, N_MOD * D), fixed)]
    args = [hs, g.reshape(1, D), ml, mc]
    if mode == "glu":
        in_specs += [pl.BlockSpec((D, tn), lambda i, j: (0, j)),
                     pl.BlockSpec((D, tn), lambda i, j: (0, j + nj)),
                     pl.BlockSpec((1, tn), lambda i, j: (0, j)),
                     pl.BlockSpec((1, tn), lambda i, j: (0, j + nj))]
        b2 = bias.reshape(1, N)
        args += [w, w, b2, b2]
    else:
        in_specs.append(pl.BlockSpec((D, tn), lambda i, j: (0, j)))
        args.append(w)
        if mode == "rope":
            tab = pl.BlockSpec((tm, LANE), lambda i, j: (i % tpb, 0))
            in_specs += [tab, tab, tab]
            args += list(rope)
    out_specs = pl.BlockSpec((tm, tn), lambda i, j: (i, j))
    out_shape = jax.ShapeDtypeStruct((M, n_out), ACT)
    if mode == "gla":
        in_specs.append(pl.BlockSpec((D, LANE), fixed))
        args.append(wa1)
        out_specs = [out_specs, pl.BlockSpec((tm, LANE), row)]
        out_shape = [out_shape, jax.ShapeDtypeStruct((M, LANE), ACT)]
    kern = functools.partial(_proj_kernel, mode=mode, tpb=tpb, n_ctx=n_ctx,
                             n_rope_tiles=n_rope_cols // tn, q_tiles=n_q_cols // tn,
                             q_scale=q_scale)
    return pl.pallas_call(
        kern,
        grid=(M // tm, nj),
        in_specs=in_specs,
        out_specs=out_specs,
        out_shape=out_shape,
        scratch_shapes=[pltpu.VMEM((tm, D), ACT)],
        compiler_params=_params(2),
        name="project_" + mode,
    )(*args)


def _mlp_kernel(h_ref, g_ref, ml_ref, mc_ref, w1_ref, w2_ref, o_ref, u_scr, acc_scr,
                *, tpb, n_ctx):
    i = pl.program_id(0)
    j = pl.program_id(1)
    tm, D = h_ref.shape
    first_tile = i % tpb == 0

    @pl.when(j == 0)
    def _():
        def store(u):
            u_scr[...] = u.astype(ACT)
        _norm_mod_store(h_ref, g_ref, ml_ref, mc_ref, 3, 4, first_tile, n_ctx, store)
        acc_scr[...] = jnp.zeros_like(acc_scr)

    hid = jnp.maximum(_dot(u_scr[...], w1_ref[...]), 0.0)
    acc_scr[...] += _dot((hid * hid).astype(ACT), w2_ref[...])

    @pl.when(j == pl.num_programs(1) - 1)
    def _():
        gate = _mod_pick(ml_ref, mc_ref, 5, D, _ctx_rows(tm, n_ctx, first_tile))
        o_ref[...] = h_ref[...] + gate * acc_scr[...]


def _mlp(hs, g, ml, mc, w1, w2, layer, *, L, n_ctx):
    M, D = hs.shape
    Hd = w1.shape[2]
    tm = _row_tile(L, n_ctx, 768)
    tpb = L // tm
    th = _col_tile(Hd, 512)
    row = lambda i, j: (i, 0)
    fixed = lambda i, j: (0, 0)
    return pl.pallas_call(
        functools.partial(_mlp_kernel, tpb=tpb, n_ctx=n_ctx),
        grid=(M // tm, Hd // th),
        in_specs=[pl.BlockSpec((tm, D), row),
                  pl.BlockSpec((1, D), fixed),
                  pl.BlockSpec((None, 1, N_MOD * D), lambda i, j: (i // tpb, 0, 0)),
                  pl.BlockSpec((1, N_MOD * D), fixed),
                  pl.BlockSpec((None, D, th), lambda i, j: (layer, 0, j)),
                  pl.BlockSpec((None, th, D), lambda i, j: (layer, j, 0))],
        out_specs=pl.BlockSpec((tm, D), row),
        out_shape=jax.ShapeDtypeStruct((M, D), F32),
        scratch_shapes=[pltpu.VMEM((tm, D), ACT), pltpu.VMEM((tm, D), F32)],
        compiler_params=_params(2),
        name="mlp",
    )(hs, g.reshape(1, D), ml, mc, w1, w2)


def _out_kernel(*refs, mode, tpb, n_ctx, n_heads):
    if mode == "gla":
        h_ref, ml_ref, mc_ref, w_ref, of_ref, ob_ref, gt_ref, ng_ref, o_ref = refs
    else:
        h_ref, ml_ref, mc_ref, w_ref, a_ref, o_ref = refs
    i = pl.program_id(0)
    tm, D = h_ref.shape
    ctx_row = _ctx_rows(tm, n_ctx, i % tpb == 0)
    if mode == "gla":
        o = of_ref[...].astype(F32) + ob_ref[...].astype(F32)
        dv = o.shape[1] // n_heads
        ng = ng_ref[...]
        o = jnp.concatenate([_rms(o[:, hh * dv:(hh + 1) * dv]) * ng for hh in range(n_heads)],
                            axis=1)
        gt = gt_ref[...].astype(F32)
        a = (o * (gt * _sigmoid(gt))).astype(ACT)
    else:
        a = a_ref[...]
    gate = _mod_pick(ml_ref, mc_ref, 2, D, ctx_row)
    o_ref[...] = h_ref[...] + gate * _dot(a, w_ref[...])


def _out_residual(hs, ml, mc, w, *, L, n_ctx, a=None, gla=None):
    M, D = hs.shape
    K = w.shape[0]
    tm = _row_tile(L, n_ctx, 256)
    tpb = L // tm
    row = lambda i: (i, 0)
    fixed = lambda i: (0, 0)
    in_specs = [pl.BlockSpec((tm, D), row),
                pl.BlockSpec((None, 1, N_MOD * D), lambda i: (i // tpb, 0, 0)),
                pl.BlockSpec((1, N_MOD * D), fixed),
                pl.BlockSpec((K, D), fixed)]
    args = [hs, ml, mc, w]
    if gla is not None:
        o_f, o_b, qkvg, norm_g = gla
        g_blk = qkvg.shape[1] // K - 1
        in_specs += [pl.BlockSpec((tm, K), row), pl.BlockSpec((tm, K), row),
                     pl.BlockSpec((tm, K), lambda i: (i, g_blk)),
                     pl.BlockSpec((1, norm_g.shape[0]), fixed)]
        args += [o_f, o_b, qkvg, norm_g.reshape(1, -1)]
        mode = "gla"
    else:
        in_specs.append(pl.BlockSpec((tm, K), row))
        args.append(a)
        mode = "plain"
    return pl.pallas_call(
        functools.partial(_out_kernel, mode=mode, tpb=tpb, n_ctx=n_ctx, n_heads=GLA_HEADS),
        grid=(M // tm,),
        in_specs=in_specs,
        out_specs=pl.BlockSpec((tm, D), row),
        out_shape=jax.ShapeDtypeStruct((M, D), F32),
        compiler_params=_params(1),
        name="out_" + mode,
    )(*args)


def _gla_direction(t_ref, wa2_ref, ba_ref, q_ref, k_ref, v_ref, o_ref, s_scr,
                   la_scr, qd_scr, qa_scr, qf_scr, kr_scr, km_scr, oi_scr, af_scr,
                   *, forward, q_scale):
    C = GLA_CHUNK
    blk = q_ref.shape[0]
    n = blk // C
    dk, dv = s_scr.shape
    order = list(range(n)) if forward else list(range(n - 1, -1, -1))
    last_row = C - 1 if forward else 0
    r = lax.broadcasted_iota(jnp.int32, (blk, blk), 0)
    c = lax.broadcasted_iota(jnp.int32, (blk, blk), 1)
    causal = (c <= r) if forward else (c >= r)
    shift = C.bit_length() - 1
    same_chunk = jnp.right_shift(r, shift) == jnp.right_shift(c, shift)
    tri = jnp.logical_and(causal, same_chunk).astype(ACT)

    z = _dot(t_ref[...], wa2_ref[...]) + ba_ref[...]
    g = (jnp.minimum(z, 0.0) - jnp.log(1.0 + jnp.exp(-jnp.abs(z)))) / GLA_GATE_NORM
    g_hi = g.astype(ACT)
    g_lo = (g - g_hi.astype(F32)).astype(ACT)
    la_scr[...] = _dot(tri, g_hi) + _dot(tri, g_lo)

    totals = []
    for ci in range(n):
        rows = pl.ds(ci * C, C)
        b = la_scr[rows, :]
        total = b[last_row:last_row + 1, :]
        totals.append(total)
        q = q_ref[rows, :].astype(F32) * q_scale
        k = k_ref[rows, :].astype(F32)
        q_dec = q * jnp.exp(b)
        qd_scr[rows, :] = q_dec
        qa_scr[rows, :] = q_dec.astype(ACT)
        km_scr[rows, :] = (k * jnp.exp(-b)).astype(ACT)
        kr_scr[rows, :] = k * jnp.exp(total - b)

    run = None
    for ci in order:
        rows = pl.ds(ci * C, C)
        q_dec = qd_scr[rows, :]
        qf_scr[rows, :] = (q_dec if run is None else q_dec * jnp.exp(run)).astype(ACT)
        run = totals[ci] if run is None else run + totals[ci]
    state = s_scr[...]
    oi_scr[...] = _dot(qf_scr[...], state.astype(ACT))

    def keys_at(pos, cj):
        gap = None
        for cm in order[order.index(cj) + 1:pos]:
            gap = totals[cm] if gap is None else gap + totals[cm]
        k_end = kr_scr[pl.ds(cj * C, C), :]
        return k_end if gap is None else k_end * jnp.exp(gap)

    for pos, ci in enumerate(order):
        rows = pl.ds(ci * C, C)
        keys = jnp.concatenate(
            [keys_at(pos, cj).astype(ACT) if cj in order[:pos] else km_scr[pl.ds(cj * C, C), :]
             for cj in range(n)], axis=0)
        att = _dot_nt(qa_scr[rows, :], keys)
        af_scr[rows, :] = jnp.where(causal[ci * C:(ci + 1) * C, :], att, 0.0).astype(ACT)

    k_fin = jnp.concatenate([keys_at(n, cj) for cj in range(n)], axis=0)
    both = _dot(jnp.concatenate([af_scr[...], k_fin.T.astype(ACT)], axis=0), v_ref[...])
    o_ref[...] = (both[:blk, :] + oi_scr[...]).astype(o_ref.dtype)
    decay_col = jnp.broadcast_to(jnp.exp(run), (LANE, dk)).T
    s_scr[...] = state * jnp.tile(decay_col, (1, dv // LANE)) + both[blk:, :]


def _gla_kernel(tf_ref, w2f_ref, baf_ref, qf_ref, kf_ref, vf_ref,
                tb_ref, w2b_ref, bab_ref, qb_ref, kb_ref, vb_ref,
                of_ref, ob_ref, sf_scr, sb_scr, *scratch, q_scale):
    @pl.when(pl.program_id(2) == 0)
    def _():
        sf_scr[...] = jnp.zeros_like(sf_scr)
        sb_scr[...] = jnp.zeros_like(sb_scr)

    half = len(scratch) // 2
    _gla_direction(tf_ref, w2f_ref, baf_ref, qf_ref, kf_ref, vf_ref, of_ref, sf_scr,
                   *scratch[:half], forward=True, q_scale=q_scale)
    _gla_direction(tb_ref, w2b_ref, bab_ref, qb_ref, kb_ref, vb_ref, ob_ref, sb_scr,
                   *scratch[half:], forward=False, q_scale=q_scale)


def _gla_scan(qkvg, t, wa2, ba, *, B, L, n_ctx, dk, dv):
    M = qkvg.shape[0]
    H = GLA_HEADS
    blk = n_ctx
    nb = L // blk
    kq = (H * dk) // dk
    kv = (2 * H * dk) // dv

    def fwd(n):
        return n

    def bwd(n):
        return jnp.where(n == 0, 0, nb - n)

    def direction_specs(d, order):
        rows = lambda b, n: b * nb + order(n)
        return [pl.BlockSpec((blk, LANE), lambda b, h, n: (rows(b, n), 0)),
                pl.BlockSpec((None, LANE, dk), lambda b, h, n: (d, 0, h)),
                pl.BlockSpec((None, 1, dk), lambda b, h, n: (d, 0, h)),
                pl.BlockSpec((blk, dk), lambda b, h, n: (rows(b, n), h)),
                pl.BlockSpec((blk, dk), lambda b, h, n: (rows(b, n), kq + h)),
                pl.BlockSpec((blk, dv), lambda b, h, n: (rows(b, n), kv + h))]

    out_spec = lambda order: pl.BlockSpec((blk, dv), lambda b, h, n: (b * nb + order(n), h))
    stage = [pltpu.VMEM((blk, dk), F32), pltpu.VMEM((blk, dk), F32), pltpu.VMEM((blk, dk), ACT),
             pltpu.VMEM((blk, dk), ACT), pltpu.VMEM((blk, dk), F32), pltpu.VMEM((blk, dk), ACT),
             pltpu.VMEM((blk, dv), F32), pltpu.VMEM((blk, blk), ACT)]
    return pl.pallas_call(
        functools.partial(_gla_kernel, q_scale=dk ** -0.5),
        grid=(B, H, nb),
        in_specs=direction_specs(0, fwd) + direction_specs(1, bwd),
        out_specs=[out_spec(fwd), out_spec(bwd)],
        out_shape=[jax.ShapeDtypeStruct((M, H * dv), ACT)] * 2,
        scratch_shapes=[pltpu.VMEM((dk, dv), F32), pltpu.VMEM((dk, dv), F32)] + stage + stage,
        compiler_params=_params(3),
        name="gla_scan",
    )(t, wa2, ba, qkvg, qkvg, qkvg, t, wa2, ba, qkvg, qkvg, qkvg)


def _conv_kernel(h_ref, ml_ref, mc_ref, x_ref, prev_ref, next_ref, dw_ref, dwb_ref,
                 lng_ref, lnb_ref, w_ref, b_ref, o_ref, xw_scr, y_scr, *, tpb):
    i = pl.program_id(0)
    tm, D = h_ref.shape
    t = i % tpb
    has_prev = t > 1
    has_next = jnp.logical_and(t > 0, t < tpb - 1)
    xw_scr[0, 0:CONV_HALO, :] = jnp.where(has_prev, prev_ref[...].astype(F32), 0.0)
    xw_scr[0, CONV_HALO:CONV_HALO + tm, :] = x_ref[...].astype(F32)
    xw_scr[0, CONV_HALO + tm:, :] = jnp.where(has_next, next_ref[...].astype(F32), 0.0)
    n_shift = tm + 2 * CONV_HALO - SUBLANE
    for r in range(1, SUBLANE):
        xw_scr[r, 0:n_shift, :] = xw_scr[0, r:r + n_shift, :]
    first = CONV_HALO - CONV_WIDTH // 2
    rb = min(tm, 128)

    def col_body(cb, carry):
        cols = pl.ds(pl.multiple_of(cb * LANE, LANE), LANE)
        for r0 in range(0, tm, rb):
            acc = jnp.zeros((rb, LANE), F32)
            for k in range(CONV_WIDTH):
                a, r = divmod(first + k, SUBLANE)
                tap = xw_scr[r, pl.ds(r0 + a * SUBLANE, rb), cols]
                acc = acc + tap * dw_ref[pl.ds(k, 1), cols]
            y_scr[pl.ds(r0, rb), cols] = acc + dwb_ref[:, cols]
        return carry

    lax.fori_loop(0, D // LANE, col_body, 0)
    y = y_scr[...]
    yc = y - jnp.mean(y, axis=-1, keepdims=True)
    yn = yc * lax.rsqrt(jnp.mean(yc * yc, axis=-1, keepdims=True) + EPS)
    yn = yn * lng_ref[...] + lnb_ref[...]
    a = (yn * _sigmoid(yn)).astype(ACT)
    gate = _mod_pick(ml_ref, mc_ref, 2, D, t == 0)
    o_ref[...] = h_ref[...] + gate * (_dot(a, w_ref[...]) + b_ref[...])


def _conv_out(hs, ml, mc, glu, dw, dwb, ln_g, ln_b, w2, b2, *, L, n_ctx):
    M, D = hs.shape
    tm = n_ctx
    tpb = L // tm
    hb = tm // CONV_HALO
    n_halo = M // CONV_HALO
    row = lambda i: (i, 0)
    fixed = lambda i: (0, 0)
    vec = pl.BlockSpec((1, D), fixed)
    return pl.pallas_call(
        functools.partial(_conv_kernel, tpb=tpb),
        grid=(M // tm,),
        in_specs=[pl.BlockSpec((tm, D), row),
                  pl.BlockSpec((None, 1, N_MOD * D), lambda i: (i // tpb, 0, 0)),
                  pl.BlockSpec((1, N_MOD * D), fixed),
                  pl.BlockSpec((tm, D), row),
                  pl.BlockSpec((CONV_HALO, D), lambda i: (jnp.maximum(i * hb - 1, 0), 0)),
                  pl.BlockSpec((CONV_HALO, D), lambda i: (jnp.minimum((i + 1) * hb, n_halo - 1), 0)),
                  pl.BlockSpec((CONV_WIDTH, D), fixed),
                  vec, vec, vec,
                  pl.BlockSpec((D, D), fixed),
                  vec],
        out_specs=pl.BlockSpec((tm, D), row),
        out_shape=jax.ShapeDtypeStruct((M, D), F32),
        scratch_shapes=[pltpu.VMEM((SUBLANE, tm + 2 * CONV_HALO, D), F32),
                        pltpu.VMEM((tm, D), F32)],
        compiler_params=_params(1),
        name="conv_out",
    )(hs, ml, mc, glu, glu, glu, dw, dwb.reshape(1, D), ln_g.reshape(1, D), ln_b.reshape(1, D),
      w2, b2.reshape(1, D))


def _attn_kernel(q_ref, k_ref, v_ref, lq1_ref, lk1_ref, lq2_ref, lk2_ref, sg_ref, o_ref,
                 vtc_scr, vtl_scr, acc_scr, m_scr, l_scr, s0_scr, s1_scr, p0_scr, p1_scr,
                 a0_scr, a1_scr, x0_scr, x1_scr, *, n_ctx, tk, lambda_init):
    qi = pl.program_id(2)
    n_lat_steps = vtl_scr.shape[0]

    @pl.when(qi == 0)
    def _():
        vtc_scr[...] = v_ref[0:n_ctx, :].astype(F32).T.astype(ACT)

        def fill(t, carry):
            rows = pl.ds(pl.multiple_of(n_ctx + t * tk, n_ctx), tk)
            vtl_scr[t] = v_ref[rows, :].astype(F32).T.astype(ACT)
            return carry
        lax.fori_loop(0, n_lat_steps, fill, 0)

    acc_scr[...] = jnp.zeros_like(acc_scr)
    m_scr[...] = jnp.full_like(m_scr, -1e30)
    l_scr[...] = jnp.zeros_like(l_scr)

    def step(k0, size, vt):
        for j in range(2):
            lanes = slice(j * DIFF_DK, (j + 1) * DIFF_DK)
            st = _dot_nt(k_ref[pl.ds(k0, size), lanes], q_ref[:, lanes])
            m_prev = m_scr[j]
            m_new = jnp.maximum(m_prev, jnp.max(st, axis=0, keepdims=True))
            alpha = jnp.exp2(m_prev - m_new)
            p = jnp.exp2(st - m_new)
            l_scr[j] = alpha * l_scr[j] + jnp.sum(p, axis=0, keepdims=True)
            acc_scr[j] = alpha * acc_scr[j] + _dot(vt, p.astype(ACT))
            m_scr[j] = m_new

    step(0, n_ctx, vtc_scr[...])

    def scores(t, s_scr, x_scr):
        for j in range(2):
            lanes = slice(j * DIFF_DK, (j + 1) * DIFF_DK)
            part = None
            for c0 in range(0, tk, ATTN_CHUNK):
                rows = pl.ds(pl.multiple_of(n_ctx + t * tk + c0, ATTN_CHUNK), ATTN_CHUNK)
                st = _dot_nt(k_ref[rows, lanes], q_ref[:, lanes])
                s_scr[j, c0:c0 + ATTN_CHUNK, :] = st
                part = st if part is None else jnp.maximum(part, st)
            x_scr[j] = jnp.max(part, axis=0, keepdims=True)

    def softmax(s_scr, x_scr, p_scr, a_scr):
        for j in range(2):
            m_prev = m_scr[j]
            m_new = jnp.maximum(m_prev, x_scr[j])
            alpha = jnp.exp2(m_prev - m_new)
            part = None
            for c0 in range(0, tk, ATTN_CHUNK):
                p = jnp.exp2(s_scr[j, c0:c0 + ATTN_CHUNK, :] - m_new)
                p_scr[j, c0:c0 + ATTN_CHUNK, :] = p.astype(ACT)
                part = p if part is None else part + p
            l_scr[j] = alpha * l_scr[j] + jnp.sum(part, axis=0, keepdims=True)
            m_scr[j] = m_new
            a_scr[j] = alpha

    def values(t, p_scr, a_scr):
        vt = vtl_scr[t]
        for j in range(2):
            acc_scr[j] = a_scr[j] * acc_scr[j] + _dot(vt, p_scr[j])

    @pl.when(qi > 0)
    def _():
        scores(0, s0_scr, x0_scr)
        scores(1, s1_scr, x1_scr)
        softmax(s0_scr, x0_scr, p0_scr, a0_scr)

        def body(i, carry):
            t = 2 * i
            scores(t + 2, s0_scr, x0_scr)
            softmax(s1_scr, x1_scr, p1_scr, a1_scr)
            values(t, p0_scr, a0_scr)
            scores(t + 3, s1_scr, x1_scr)
            softmax(s0_scr, x0_scr, p0_scr, a0_scr)
            values(t + 1, p1_scr, a1_scr)
            return carry
        lax.fori_loop(0, n_lat_steps // 2 - 1, body, 0, unroll=True)
        softmax(s1_scr, x1_scr, p1_scr, a1_scr)
        values(n_lat_steps - 2, p0_scr, a0_scr)
        values(n_lat_steps - 1, p1_scr, a1_scr)

    lam = (jnp.exp(jnp.sum(lq1_ref[...] * lk1_ref[...], axis=-1, keepdims=True))
           - jnp.exp(jnp.sum(lq2_ref[...] * lk2_ref[...], axis=-1, keepdims=True)) + lambda_init)
    ot = acc_scr[0] * (1.0 / l_scr[0]) - lam * (acc_scr[1] * (1.0 / l_scr[1]))
    ot = ot * lax.rsqrt(jnp.mean(ot * ot, axis=0, keepdims=True) + EPS)
    ot = ot * (sg_ref[...] * (1.0 - lambda_init))
    o_ref[...] = ot.T.astype(o_ref.dtype)


def _diff_attention(qkv, lq1, lk1, lq2, lk2, subln_g, *, B, L, n_ctx, lambda_init):
    M, N = qkv.shape
    width = 2 * DIFF_DK
    H = N // (3 * width)
    tq = n_ctx
    nq = L // tq
    tk = _row_tile(L - n_ctx, n_ctx, 512)
    n_steps = (L - n_ctx) // tk
    assert n_steps >= 2 and n_steps % 2 == 0
    stage = lambda dtype: pltpu.VMEM((2, tk, tq), dtype)
    stat = pltpu.VMEM((2, 1, tq), F32)
    fixed = lambda b, h, i: (0, 0)
    vec = pl.BlockSpec((1, DIFF_DK), fixed)
    return pl.pallas_call(
        functools.partial(_attn_kernel, n_ctx=n_ctx, tk=tk, lambda_init=lambda_init),
        grid=(B, H, nq),
        in_specs=[pl.BlockSpec((tq, width), lambda b, h, i: (b * nq + i, h)),
                  pl.BlockSpec((L, width), lambda b, h, i: (b, H + h)),
                  pl.BlockSpec((L, DIFF_DV), lambda b, h, i: (b, 2 * H + h)),
                  vec, vec, vec, vec,
                  pl.BlockSpec((DIFF_DV, 1), fixed)],
        out_specs=pl.BlockSpec((tq, DIFF_DV), lambda b, h, i: (b * nq + i, h)),
        out_shape=jax.ShapeDtypeStruct((M, H * DIFF_DV), ACT),
        scratch_shapes=[pltpu.VMEM((DIFF_DV, n_ctx), ACT),
                        pltpu.VMEM((n_steps, DIFF_DV, tk), ACT),
                        pltpu.VMEM((2, DIFF_DV, tq), F32), stat, stat,
                        stage(F32), stage(F32), stage(ACT), stage(ACT), stat, stat, stat, stat],
        compiler_params=_params(3),
        name="diff_attention",
    )(qkv, qkv, qkv, lq1.reshape(1, -1), lk1.reshape(1, -1), lq2.reshape(1, -1),
      lk2.reshape(1, -1), subln_g.reshape(-1, 1))


def _rope_tables(n_ctx, n_lat):
    axis_dim = DIFF_DK // 2
    rows = n_lat // GRID_W
    pos_row = jnp.repeat(jnp.arange(rows, dtype=F32), GRID_W)
    pos_col = jnp.tile(jnp.arange(GRID_W, dtype=F32), rows)
    inv_freq = ROPE_BASE ** (-jnp.arange(0, axis_dim, 2, dtype=F32) / axis_dim)
    ang_r = pos_row[:, None] * inv_freq
    ang_c = pos_col[:, None] * inv_freq
    ang = jnp.concatenate([ang_r, ang_r, ang_c, ang_c], axis=-1)
    cos = jnp.concatenate([jnp.ones((n_ctx, DIFF_DK), F32), jnp.cos(ang)], axis=0)
    sin = jnp.concatenate([jnp.zeros((n_ctx, DIFF_DK), F32), jnp.sin(ang)], axis=0)
    first_half = (jnp.arange(DIFF_DK) % (axis_dim)) < axis_dim // 2
    return cos, jnp.where(first_half, -sin, 0.0), jnp.where(first_half, 0.0, sin)


def _final_kernel(h_ref, g_ref, o_ref):
    o_ref[...] = _rms(h_ref[...]) * g_ref[...]


def _final_norm(hs, g, *, B, L, n_ctx):
    D = hs.shape[1]
    tm = n_ctx
    skip = n_ctx // tm
    n_lat = L - n_ctx
    return pl.pallas_call(
        _final_kernel,
        grid=(B, n_lat // tm),
        in_specs=[pl.BlockSpec((None, tm, D), lambda b, i: (b, i + skip, 0)),
                  pl.BlockSpec((1, D), lambda b, i: (0, 0))],
        out_specs=pl.BlockSpec((None, tm, D), lambda b, i: (b, i, 0)),
        out_shape=jax.ShapeDtypeStruct((B, n_lat, D), F32),
        compiler_params=_params(2),
        name="final_norm",
    )(hs.reshape(B, L, D), g.reshape(1, D))


def kernel(x, c, ctx, c_ctx, mod_w, mod_b, norm_mix_g, norm_mlp_g, mlp_w1, mlp_w2, gla_w_in, gla_wa1_f, gla_wa2_f, gla_ba_f, gla_wa1_b, gla_wa2_b, gla_ba_b, gla_norm_g, gla_w_o, conv_w1, conv_b1, conv_dw, conv_dwb, conv_ln_g, conv_ln_b, conv_w2, conv_b2, diff_w_qkv, diff_lq1, diff_lk1, diff_lq2, diff_lk2, diff_subln_g, diff_w_o, final_g):
    B, n_lat, D = x.shape
    n_ctx = ctx.shape[1]
    L = n_ctx + n_lat
    depth = mod_w.shape[0]
    assert B + 1 <= MOD_ROWS and n_lat % n_ctx == 0 and n_ctx % GLA_CHUNK == 0
    dims = dict(L=L, n_ctx=n_ctx)

    hs = jnp.concatenate([ctx, x], axis=1).reshape(B * L, D)
    cvec = jnp.zeros((MOD_ROWS, D), F32).at[:B].set(c).at[B].set(c_ctx)
    mods = _mod_vectors(cvec, mod_w, mod_b)
    rope = _rope_tables(n_ctx, n_lat)
    w1_all, w2_all = mlp_w1.astype(ACT), mlp_w2.astype(ACT)

    for i in range(depth):
        ml = mods[i, :B][:, None, :]
        mc = mods[i, B:B + 1]
        kind, j = i % 3, i // 3
        if kind == 0:
            hk = gla_wa2_f.shape[2]
            dk, dv = hk // GLA_HEADS, gla_w_o.shape[1] // GLA_HEADS
            wa1 = jnp.zeros((D, LANE), F32).at[:, :GLA_RANK].set(gla_wa1_f[j])
            wa1 = wa1.at[:, GLA_RANK:2 * GLA_RANK].set(gla_wa1_b[j])
            wa2 = jnp.zeros((2, LANE, hk), F32).at[0, :GLA_RANK].set(gla_wa2_f[j])
            wa2 = wa2.at[1, GLA_RANK:2 * GLA_RANK].set(gla_wa2_b[j])
            ba = jnp.stack([gla_ba_f[j], gla_ba_b[j]])[:, None, :]
            qkvg, t = _project(hs, norm_mix_g[i], ml, mc, gla_w_in[j].astype(ACT), mode="gla",
                               wa1=wa1.astype(ACT), **dims)
            o_f, o_b = _gla_scan(qkvg, t, wa2.astype(ACT), ba, B=B, dk=dk, dv=dv, **dims)
            hs = _out_residual(hs, ml, mc, gla_w_o[j].astype(ACT),
                               gla=(o_f, o_b, qkvg, gla_norm_g[j]), **dims)
        elif kind == 1:
            glu = _project(hs, norm_mix_g[i], ml, mc, conv_w1[j].astype(ACT), mode="glu",
                           bias=conv_b1[j], **dims)
            hs = _conv_out(hs, ml, mc, glu, conv_dw[j], conv_dwb[j], conv_ln_g[j], conv_ln_b[j],
                           conv_w2[j].astype(ACT), conv_b2[j], **dims)
        else:
            lambda_init = 0.8 - 0.6 * math.exp(-0.3 * i)
            n_qk = diff_w_qkv.shape[2] // 3
            qkv = _project(hs, norm_mix_g[i], ml, mc, diff_w_qkv[j].astype(ACT), mode="rope",
                           rope=rope, n_rope_cols=2 * n_qk, n_q_cols=n_qk,
                           q_scale=DIFF_DK ** -0.5 * math.log2(math.e), **dims)
            att = _diff_attention(qkv, diff_lq1[j], diff_lk1[j], diff_lq2[j], diff_lk2[j],
                                  diff_subln_g[j], B=B, lambda_init=lambda_init, **dims)
            hs = _out_residual(hs, ml, mc, diff_w_o[j].astype(ACT), a=att, **dims)
        hs = _mlp(hs, norm_mlp_g[i], ml, mc, w1_all, w2_all, i, **dims)
    return _final_norm(hs, final_g, B=B, L=L, n_ctx=n_ctx)
```

```python
import functools
import math

import jax
import jax.numpy as jnp
from jax import lax
from jax.experimental import pallas as pl
from jax.experimental.pallas import tpu as pltpu

F32 = jnp.float32
ACT = jnp.bfloat16
EPS = 1e-6
N_MOD = 6
GLA_HEADS = 4
GLA_RANK = 16
GLA_GATE_NORM = 16.0
GLA_CHUNK = 64
CONV_WIDTH = 31
CONV_HALO = 16
DIFF_DK = 128
DIFF_DV = 256
GRID_W = 64
ROPE_BASE = 10000.0
LANE = 128
SUBLANE = 8
MOD_ROWS = 8
VMEM_LIMIT = 56 * 2 ** 20
ATTN_CHUNK = 256

def _params(n_axes):
    return pltpu.CompilerParams(dimension_semantics=("arbitrary",) * n_axes,
                                vmem_limit_bytes=VMEM_LIMIT)


def _dot(a, b):
    return jnp.dot(a, b, preferred_element_type=F32)


def _dot_nt(a, b):
    return lax.dot_general(a, b, (((1,), (1,)), ((), ())), preferred_element_type=F32)


def _dot_tn(a, b):
    return lax.dot_general(a, b, (((0,), (0,)), ((), ())), preferred_element_type=F32)


def _sigmoid(x):
    return 1.0 / (1.0 + jnp.exp(-x))


def _rms(x):
    return x * lax.rsqrt(jnp.mean(x * x, axis=-1, keepdims=True) + EPS)


def _row_tile(L, n_ctx, cap):
    best = n_ctx
    for m in range(1, L // n_ctx + 1):
        t = m * n_ctx
        if t <= cap and L % t == 0:
            best = t
    return best


def _col_tile(n, cap):
    best = LANE
    t = LANE
    while t <= min(n, cap):
        if n % t == 0:
            best = t
        t += LANE
    return best


def _mod_pick(ml_ref, mc_ref, k, D, ctx_row):
    lat = ml_ref[:, k * D:(k + 1) * D]
    ctx = mc_ref[:, k * D:(k + 1) * D]
    return jnp.where(ctx_row, ctx, lat)


def _ctx_rows(tm, n_ctx, first_tile):
    rows = lax.broadcasted_iota(jnp.int32, (tm, 1), 0)
    return jnp.logical_and(first_tile, rows < n_ctx)


def _norm_mod_store(h_ref, g_ref, ml_ref, mc_ref, k_shift, k_scale, first_tile, n_ctx, store):
    h = h_ref[...]
    tm, D = h.shape
    y = _rms(h)
    g = g_ref[...]
    lat_a = g * (1.0 + ml_ref[:, k_scale * D:(k_scale + 1) * D])
    lat_b = ml_ref[:, k_shift * D:(k_shift + 1) * D]

    @pl.when(jnp.logical_not(first_tile))
    def _():
        store(y * lat_a + lat_b)

    @pl.when(first_tile)
    def _():
        ctx_row = lax.broadcasted_iota(jnp.int32, (tm, 1), 0) < n_ctx
        ctx_a = g * (1.0 + mc_ref[:, k_scale * D:(k_scale + 1) * D])
        ctx_b = mc_ref[:, k_shift * D:(k_shift + 1) * D]
        store(y * jnp.where(ctx_row, ctx_a, lat_a) + jnp.where(ctx_row, ctx_b, lat_b))


def _mod_kernel(c_ref, w_ref, b_ref, o_ref):
    c = c_ref[...]
    sc = (c * _sigmoid(c)).astype(ACT)
    o_ref[...] = _dot(sc, w_ref[...].astype(ACT)) + b_ref[...]


def _mod_vectors(cvec, mod_w, mod_b):
    depth, D, N = mod_w.shape
    tn = _col_tile(N, 1024)
    return pl.pallas_call(
        _mod_kernel,
        grid=(depth, N // tn),
        in_specs=[pl.BlockSpec((MOD_ROWS, D), lambda l, j: (0, 0)),
                  pl.BlockSpec((None, D, tn), lambda l, j: (l, 0, j)),
                  pl.BlockSpec((None, 1, tn), lambda l, j: (l, 0, j))],
        out_specs=pl.BlockSpec((None, MOD_ROWS, tn), lambda l, j: (l, 0, j)),
        out_shape=jax.ShapeDtypeStruct((depth, MOD_ROWS, N), F32),
        compiler_params=_params(2),
        name="mod_vectors",
    )(cvec, mod_w, mod_b.reshape(depth, 1, N))


def _rope_tile(acc, cos, sin):
    return acc * cos + pltpu.roll(acc, DIFF_DK // 2, axis=1) * sin


def _proj_kernel(*refs, mode, tpb, n_ctx, n_rope_tiles, q_tiles, q_scale):
    if mode == "glu":
        h_ref, g_ref, ml_ref, mc_ref, wa_ref, wg_ref, ba_ref, bg_ref, o_ref, u_scr = refs
    elif mode == "rope":
        h_ref, g_ref, ml_ref, mc_ref, w_ref, cos_ref, sin_ref, o_ref, u_scr = refs
    else:
        h_ref, g_ref, ml_ref, mc_ref, w_ref, wa1_ref, o_ref, t_ref, u_scr = refs
    i = pl.program_id(0)
    j = pl.program_id(1)

    @pl.when(j == 0)
    def _():
        def store(u):
            u_scr[...] = u.astype(ACT)
        _norm_mod_store(h_ref, g_ref, ml_ref, mc_ref, 0, 1, i % tpb == 0, n_ctx, store)
        if mode == "gla":
            t_ref[...] = _dot(u_scr[...], wa1_ref[...]).astype(t_ref.dtype)

    u = u_scr[...]
    if mode == "glu":
        a = _dot(u, wa_ref[...]) + ba_ref[...]
        g = _dot(u, wg_ref[...]) + bg_ref[...]
        o_ref[...] = (a * _sigmoid(g)).astype(o_ref.dtype)
    elif mode == "rope":
        acc = _dot(u, w_ref[...])

        @pl.when(j < n_rope_tiles)
        def _():
            scale = jnp.where(j < q_tiles, q_scale, 1.0)
            cos, sin = cos_ref[...], sin_ref[...]
            for s in range(acc.shape[1] // LANE):
                seg = acc[:, s * LANE:(s + 1) * LANE]
                o_ref[:, s * LANE:(s + 1) * LANE] = (
                    _rope_tile(seg, cos, sin) * scale).astype(o_ref.dtype)

        @pl.when(j >= n_rope_tiles)
        def _():
            o_ref[...] = acc.astype(o_ref.dtype)
    else:
        o_ref[...] = _dot(u, w_ref[...]).astype(o_ref.dtype)


def _project(hs, g, ml, mc, w, *, L, n_ctx, mode, bias=None, rope=None, wa1=None,
             n_rope_cols=0, n_q_cols=0, q_scale=1.0):
    M, D = hs.shape
    N = w.shape[1]
    n_out = N // 2 if mode == "glu" else N
    tm = _row_tile(L, n_ctx, 768)
    tpb = L // tm
    tn = _col_tile(n_out, 1024)
    if mode == "rope":
        while n_rope_cols % tn or n_q_cols % tn:
            tn -= LANE
    nj = n_out // tn
    row = lambda i, j: (i, 0)
    fixed = lambda i, j: (0, 0)
    in_specs = [pl.BlockSpec((tm, D), row),
                pl.BlockSpec((1, D), fixed),
                pl.BlockSpec((None, 1, N_MOD * D), lambda i, j: (i // tpb, 0, 0)),
                pl.BlockSpec((1, N_MOD * D), fixed)]
    args = [hs, g.reshape(1, D), ml, mc]
    if mode == "glu":
        in_specs += [pl.BlockSpec((D, tn), lambda i, j: (0, j)),
                     pl.BlockSpec((D, tn), lambda i, j: (0, j + nj)),
                     pl.BlockSpec((1, tn), lambda i, j: (0, j)),
                     pl.BlockSpec((1, tn), lambda i, j: (0, j + nj))]
        b2 = bias.reshape(1, N)
        args += [w, w, b2, b2]
    else:
        in_specs.append(pl.BlockSpec((D, tn), lambda i, j: (0, j)))
        args.append(w)
        if mode == "rope":
            tab = pl.BlockSpec((tm, LANE), lambda i, j: (i % tpb, 0))
            in_specs += [tab, tab]
            args += list(rope)
    out_specs = pl.BlockSpec((tm, tn), lambda i, j: (i, j))
    out_shape = jax.ShapeDtypeStruct((M, n_out), ACT)
    if mode == "gla":
        in_specs.append(pl.BlockSpec((D, LANE), fixed))
        args.append(wa1)
        out_specs = [out_specs, pl.BlockSpec((tm, LANE), row)]
        out_shape = [out_shape, jax.ShapeDtypeStruct((M, LANE), ACT)]
    kern = functools.partial(_proj_kernel, mode=mode, tpb=tpb, n_ctx=n_ctx,
                             n_rope_tiles=n_rope_cols // tn, q_tiles=n_q_cols // tn,
                             q_scale=q_scale)
    return pl.pallas_call(
        kern,
        grid=(M // tm, nj),
        in_specs=in_specs,
        out_specs=out_specs,
        out_shape=out_shape,
        scratch_shapes=[pltpu.VMEM((tm, D), ACT)],
        compiler_params=_params(2),
        name="project_" + mode,
    )(*args)


def _mlp_kernel(h_ref, g_ref, ml_ref, mc_ref, w1_ref, w2_ref, o_ref, u_scr, acc_scr,
                *, tpb, n_ctx):
    i = pl.program_id(0)
    j = pl.program_id(1)
    tm, D = h_ref.shape
    first_tile = i % tpb == 0

    @pl.when(j == 0)
    def _():
        def store(u):
            u_scr[...] = u.astype(ACT)
        _norm_mod_store(h_ref, g_ref, ml_ref, mc_ref, 3, 4, first_tile, n_ctx, store)
        acc_scr[...] = jnp.zeros_like(acc_scr)

    hid = jnp.maximum(_dot(u_scr[...], w1_ref[...]), 0.0)
    acc_scr[...] += _dot((hid * hid).astype(ACT), w2_ref[...])

    @pl.when(j == pl.num_programs(1) - 1)
    def _():
        gate = _mod_pick(ml_ref, mc_ref, 5, D, _ctx_rows(tm, n_ctx, first_tile))
        o_ref[...] = h_ref[...] + gate * acc_scr[...]


def _mlp(hs, g, ml, mc, w1, w2, layer, *, L, n_ctx):
    M, D = hs.shape
    Hd = w1.shape[2]
    tm = _row_tile(L, n_ctx, 768)
    tpb = L // tm
    th = _col_tile(Hd, 512)
    row = lambda i, j: (i, 0)
    fixed = lambda i, j: (0, 0)
    return pl.pallas_call(
        functools.partial(_mlp_kernel, tpb=tpb, n_ctx=n_ctx),
        grid=(M // tm, Hd // th),
        in_specs=[pl.BlockSpec((tm, D), row),
                  pl.BlockSpec((1, D), fixed),
                  pl.BlockSpec((None, 1, N_MOD * D), lambda i, j: (i // tpb, 0, 0)),
                  pl.BlockSpec((1, N_MOD * D), fixed),
                  pl.BlockSpec((None, D, th), lambda i, j: (layer, 0, j)),
                  pl.BlockSpec((None, th, D), lambda i, j: (layer, j, 0))],
        out_specs=pl.BlockSpec((tm, D), row),
        out_shape=jax.ShapeDtypeStruct((M, D), F32),
        scratch_shapes=[pltpu.VMEM((tm, D), ACT), pltpu.VMEM((tm, D), F32)],
        compiler_params=_params(2),
        name="mlp",
    )(hs, g.reshape(1, D), ml, mc, w1, w2)


def _out_kernel(*refs, mode, tpb, n_ctx, n_heads):
    if mode == "gla":
        h_ref, ml_ref, mc_ref, w_ref, of_ref, ob_ref, gt_ref, ng_ref, o_ref = refs
    else:
        h_ref, ml_ref, mc_ref, w_ref, a_ref, o_ref = refs
    i = pl.program_id(0)
    tm, D = h_ref.shape
    ctx_row = _ctx_rows(tm, n_ctx, i % tpb == 0)
    if mode == "gla":
        o = of_ref[...].astype(F32) + ob_ref[...].astype(F32)
        dv = o.shape[1] // n_heads
        ng = ng_ref[...]
        o = jnp.concatenate([_rms(o[:, hh * dv:(hh + 1) * dv]) * ng for hh in range(n_heads)],
                            axis=1)
        gt = gt_ref[...].astype(F32)
        a = (o * (gt * _sigmoid(gt))).astype(ACT)
    else:
        a = a_ref[...]
    gate = _mod_pick(ml_ref, mc_ref, 2, D, ctx_row)
    o_ref[...] = h_ref[...] + gate * _dot(a, w_ref[...])


def _out_residual(hs, ml, mc, w, *, L, n_ctx, a=None, gla=None):
    M, D = hs.shape
    K = w.shape[0]
    tm = _row_tile(L, n_ctx, 256)
    tpb = L // tm
    row = lambda i: (i, 0)
    fixed = lambda i: (0, 0)
    in_specs = [pl.BlockSpec((tm, D), row),
                pl.BlockSpec((None, 1, N_MOD * D), lambda i: (i // tpb, 0, 0)),
                pl.BlockSpec((1, N_MOD * D), fixed),
                pl.BlockSpec((K, D), fixed)]
    args = [hs, ml, mc, w]
    if gla is not None:
        o_f, o_b, qkvg, norm_g = gla
        g_blk = qkvg.shape[1] // K - 1
        in_specs += [pl.BlockSpec((tm, K), row), pl.BlockSpec((tm, K), row),
                     pl.BlockSpec((tm, K), lambda i: (i, g_blk)),
                     pl.BlockSpec((1, norm_g.shape[0]), fixed)]
        args += [o_f, o_b, qkvg, norm_g.reshape(1, -1)]
        mode = "gla"
    else:
        in_specs.append(pl.BlockSpec((tm, K), row))
        args.append(a)
        mode = "plain"
    return pl.pallas_call(
        functools.partial(_out_kernel, mode=mode, tpb=tpb, n_ctx=n_ctx, n_heads=GLA_HEADS),
        grid=(M // tm,),
        in_specs=in_specs,
        out_specs=pl.BlockSpec((tm, D), row),
        out_shape=jax.ShapeDtypeStruct((M, D), F32),
        compiler_params=_params(1),
        name="out_" + mode,
    )(*args)


def _gla_direction(t_ref, wa2_ref, ba_ref, q_ref, k_ref, v_ref, o_ref, s_scr,
                   la_scr, qd_scr, qa_scr, qf_scr, kr_scr, km_scr, oi_scr, af_scr,
                   *, forward, q_scale):
    C = GLA_CHUNK
    blk = q_ref.shape[0]
    n = blk // C
    dk, dv = s_scr.shape
    order = list(range(n)) if forward else list(range(n - 1, -1, -1))
    last_row = C - 1 if forward else 0
    r = lax.broadcasted_iota(jnp.int32, (blk, blk), 0)
    c = lax.broadcasted_iota(jnp.int32, (blk, blk), 1)
    causal = (c <= r) if forward else (c >= r)
    shift = C.bit_length() - 1
    same_chunk = jnp.right_shift(r, shift) == jnp.right_shift(c, shift)
    tri = jnp.logical_and(causal, same_chunk).astype(ACT)

    z = _dot(t_ref[...], wa2_ref[...]) + ba_ref[...]
    g = (jnp.minimum(z, 0.0) - jnp.log(1.0 + jnp.exp(-jnp.abs(z)))) / GLA_GATE_NORM
    g_hi = g.astype(ACT)
    g_lo = (g - g_hi.astype(F32)).astype(ACT)
    la_scr[...] = _dot(tri, g_hi) + _dot(tri, g_lo)

    totals = []
    for ci in range(n):
        rows = pl.ds(ci * C, C)
        b = la_scr[rows, :]
        total = b[last_row:last_row + 1, :]
        totals.append(total)
        q = q_ref[rows, :].astype(F32) * q_scale
        k = k_ref[rows, :].astype(F32)
        q_dec = q * jnp.exp(b)
        qd_scr[rows, :] = q_dec
        qa_scr[rows, :] = q_dec.astype(ACT)
        km_scr[rows, :] = (k * jnp.exp(-b)).astype(ACT)
        kr_scr[rows, :] = k * jnp.exp(total - b)

    run = None
    for ci in order:
        rows = pl.ds(ci * C, C)
        q_dec = qd_scr[rows, :]
        qf_scr[rows, :] = (q_dec if run is None else q_dec * jnp.exp(run)).astype(ACT)
        run = totals[ci] if run is None else run + totals[ci]
    state = s_scr[...]
    oi_scr[...] = _dot(qf_scr[...], state.astype(ACT))

    def keys_at(pos, cj):
        gap = None
        for cm in order[order.index(cj) + 1:pos]:
            gap = totals[cm] if gap is None else gap + totals[cm]
        k_end = kr_scr[pl.ds(cj * C, C), :]
        return k_end if gap is None else k_end * jnp.exp(gap)

    for pos, ci in enumerate(order):
        rows = pl.ds(ci * C, C)
        keys = jnp.concatenate(
            [keys_at(pos, cj).astype(ACT) if cj in order[:pos] else km_scr[pl.ds(cj * C, C), :]
             for cj in range(n)], axis=0)
        att = _dot_nt(qa_scr[rows, :], keys)
        af_scr[rows, :] = jnp.where(causal[ci * C:(ci + 1) * C, :], att, 0.0).astype(ACT)

    k_fin = jnp.concatenate([keys_at(n, cj) for cj in range(n)], axis=0)
    both = _dot(jnp.concatenate([af_scr[...], k_fin.T.astype(ACT)], axis=0), v_ref[...])
    o_ref[...] = (both[:blk, :] + oi_scr[...]).astype(o_ref.dtype)
    decay_col = jnp.broadcast_to(jnp.exp(run), (LANE, dk)).T
    s_scr[...] = state * jnp.tile(decay_col, (1, dv // LANE)) + both[blk:, :]


def _gla_kernel(tf_ref, w2f_ref, baf_ref, qf_ref, kf_ref, vf_ref,
                tb_ref, w2b_ref, bab_ref, qb_ref, kb_ref, vb_ref,
                of_ref, ob_ref, sf_scr, sb_scr, *scratch, q_scale):
    @pl.when(pl.program_id(2) == 0)
    def _():
        sf_scr[...] = jnp.zeros_like(sf_scr)
        sb_scr[...] = jnp.zeros_like(sb_scr)

    half = len(scratch) // 2
    _gla_direction(tf_ref, w2f_ref, baf_ref, qf_ref, kf_ref, vf_ref, of_ref, sf_scr,
                   *scratch[:half], forward=True, q_scale=q_scale)
    _gla_direction(tb_ref, w2b_ref, bab_ref, qb_ref, kb_ref, vb_ref, ob_ref, sb_scr,
                   *scratch[half:], forward=False, q_scale=q_scale)


def _gla_scan(qkvg, t, wa2, ba, *, B, L, n_ctx, dk, dv):
    M = qkvg.shape[0]
    H = GLA_HEADS
    blk = n_ctx
    nb = L // blk
    kq = (H * dk) // dk
    kv = (2 * H * dk) // dv

    def fwd(n):
        return n

    def bwd(n):
        return jnp.where(n == 0, 0, nb - n)

    def direction_specs(d, order):
        rows = lambda b, n: b * nb + order(n)
        return [pl.BlockSpec((blk, LANE), lambda b, h, n: (rows(b, n), 0)),
                pl.BlockSpec((None, LANE, dk), lambda b, h, n: (d, 0, h)),
                pl.BlockSpec((None, 1, dk), lambda b, h, n: (d, 0, h)),
                pl.BlockSpec((blk, dk), lambda b, h, n: (rows(b, n), h)),
                pl.BlockSpec((blk, dk), lambda b, h, n: (rows(b, n), kq + h)),
                pl.BlockSpec((blk, dv), lambda b, h, n: (rows(b, n), kv + h))]

    out_spec = lambda order: pl.BlockSpec((blk, dv), lambda b, h, n: (b * nb + order(n), h))
    stage = [pltpu.VMEM((blk, dk), F32), pltpu.VMEM((blk, dk), F32), pltpu.VMEM((blk, dk), ACT),
             pltpu.VMEM((blk, dk), ACT), pltpu.VMEM((blk, dk), F32), pltpu.VMEM((blk, dk), ACT),
             pltpu.VMEM((blk, dv), F32), pltpu.VMEM((blk, blk), ACT)]
    return pl.pallas_call(
        functools.partial(_gla_kernel, q_scale=dk ** -0.5),
        grid=(B, H, nb),
        in_specs=direction_specs(0, fwd) + direction_specs(1, bwd),
        out_specs=[out_spec(fwd), out_spec(bwd)],
        out_shape=[jax.ShapeDtypeStruct((M, H * dv), ACT)] * 2,
        scratch_shapes=[pltpu.VMEM((dk, dv), F32), pltpu.VMEM((dk, dv), F32)] + stage + stage,
        compiler_params=_params(3),
        name="gla_scan",
    )(t, wa2, ba, qkvg, qkvg, qkvg, t, wa2, ba, qkvg, qkvg, qkvg)


def _conv_kernel(h_ref, ml_ref, mc_ref, x_ref, prev_ref, next_ref, dw_ref, dwb_ref,
                 lng_ref, lnb_ref, w_ref, b_ref, o_ref, xw_scr, y_scr, *, tpb):
    i = pl.program_id(0)
    tm, D = h_ref.shape
    t = i % tpb
    has_prev = t > 1
    has_next = jnp.logical_and(t > 0, t < tpb - 1)
    xw_scr[0, 0:CONV_HALO, :] = jnp.where(has_prev, prev_ref[...].astype(F32), 0.0)
    xw_scr[0, CONV_HALO:CONV_HALO + tm, :] = x_ref[...].astype(F32)
    xw_scr[0, CONV_HALO + tm:, :] = jnp.where(has_next, next_ref[...].astype(F32), 0.0)
    n_shift = tm + 2 * CONV_HALO - SUBLANE
    for r in range(1, SUBLANE):
        xw_scr[r, 0:n_shift, :] = xw_scr[0, r:r + n_shift, :]
    first = CONV_HALO - CONV_WIDTH // 2
    rb = min(tm, 128)

    def col_body(cb, carry):
        cols = pl.ds(pl.multiple_of(cb * LANE, LANE), LANE)
        for r0 in range(0, tm, rb):
            acc = jnp.zeros((rb, LANE), F32)
            for k in range(CONV_WIDTH):
                a, r = divmod(first + k, SUBLANE)
                tap = xw_scr[r, pl.ds(r0 + a * SUBLANE, rb), cols]
                acc = acc + tap * dw_ref[pl.ds(k, 1), cols]
            y_scr[pl.ds(r0, rb), cols] = acc + dwb_ref[:, cols]
        return carry

    lax.fori_loop(0, D // LANE, col_body, 0)
    y = y_scr[...]
    yc = y - jnp.mean(y, axis=-1, keepdims=True)
    yn = yc * lax.rsqrt(jnp.mean(yc * yc, axis=-1, keepdims=True) + EPS)
    yn = yn * lng_ref[...] + lnb_ref[...]
    a = (yn * _sigmoid(yn)).astype(ACT)
    gate = _mod_pick(ml_ref, mc_ref, 2, D, t == 0)
    o_ref[...] = h_ref[...] + gate * (_dot(a, w_ref[...]) + b_ref[...])


def _conv_out(hs, ml, mc, glu, dw, dwb, ln_g, ln_b, w2, b2, *, L, n_ctx):
    M, D = hs.shape
    tm = n_ctx
    tpb = L // tm
    hb = tm // CONV_HALO
    n_halo = M // CONV_HALO
    row = lambda i: (i, 0)
    fixed = lambda i: (0, 0)
    vec = pl.BlockSpec((1, D), fixed)
    return pl.pallas_call(
        functools.partial(_conv_kernel, tpb=tpb),
        grid=(M // tm,),
        in_specs=[pl.BlockSpec((tm, D), row),
                  pl.BlockSpec((None, 1, N_MOD * D), lambda i: (i // tpb, 0, 0)),
                  pl.BlockSpec((1, N_MOD * D), fixed),
                  pl.BlockSpec((tm, D), row),
                  pl.BlockSpec((CONV_HALO, D), lambda i: (jnp.maximum(i * hb - 1, 0), 0)),
                  pl.BlockSpec((CONV_HALO, D), lambda i: (jnp.minimum((i + 1) * hb, n_halo - 1), 0)),
                  pl.BlockSpec((CONV_WIDTH, D), fixed),
                  vec, vec, vec,
                  pl.BlockSpec((D, D), fixed),
                  vec],
        out_specs=pl.BlockSpec((tm, D), row),
        out_shape=jax.ShapeDtypeStruct((M, D), F32),
        scratch_shapes=[pltpu.VMEM((SUBLANE, tm + 2 * CONV_HALO, D), F32),
                        pltpu.VMEM((tm, D), F32)],
        compiler_params=_params(1),
        name="conv_out",
    )(hs, ml, mc, glu, glu, glu, dw, dwb.reshape(1, D), ln_g.reshape(1, D), ln_b.reshape(1, D),
      w2, b2.reshape(1, D))


def _attn_kernel(q_ref, k_ref, v_ref, lq1_ref, lk1_ref, lq2_ref, lk2_ref, sg_ref, o_ref,
                 vtc_scr, vtl_scr, acc_scr, m_scr, l_scr, s0_scr, s1_scr, p0_scr, p1_scr,
                 a0_scr, a1_scr, x0_scr, x1_scr, *, n_ctx, tk, lambda_init):
    qi = pl.program_id(2)
    n_lat_steps = vtl_scr.shape[0]

    def context_keys():
        vt = vtc_scr[...]
        for j in range(2):
            lanes = slice(j * DIFF_DK, (j + 1) * DIFF_DK)
            st = _dot_nt(k_ref[0:n_ctx, lanes], q_ref[:, lanes])
            m = jnp.max(st, axis=0, keepdims=True)
            p = jnp.exp2(st - m)
            l_scr[j] = jnp.sum(p, axis=0, keepdims=True)
            acc_scr[j] = _dot(vt, p.astype(ACT))
            m_scr[j] = m

    def finish():
        lam = (jnp.exp(jnp.sum(lq1_ref[...] * lk1_ref[...], axis=-1, keepdims=True))
               - jnp.exp(jnp.sum(lq2_ref[...] * lk2_ref[...], axis=-1, keepdims=True))
               + lambda_init)
        ot = acc_scr[0] * (1.0 / l_scr[0]) - lam * (acc_scr[1] * (1.0 / l_scr[1]))
        ot = ot * lax.rsqrt(jnp.mean(ot * ot, axis=0, keepdims=True) + EPS)
        ot = ot * (sg_ref[...] * (1.0 - lambda_init))
        o_ref[...] = ot.T.astype(o_ref.dtype)

    def scores(t, s_scr, x_scr):
        for j in range(2):
            lanes = slice(j * DIFF_DK, (j + 1) * DIFF_DK)
            part = None
            for c0 in range(0, tk, ATTN_CHUNK):
                rows = pl.ds(pl.multiple_of(n_ctx + t * tk + c0, ATTN_CHUNK), ATTN_CHUNK)
                st = _dot_nt(k_ref[rows, lanes], q_ref[:, lanes])
                s_scr[j, c0:c0 + ATTN_CHUNK, :] = st
                part = st if part is None else jnp.maximum(part, st)
            x_scr[j] = jnp.max(part, axis=0, keepdims=True)

    def softmax(s_scr, x_scr, p_scr, a_scr):
        for j in range(2):
            m_prev = m_scr[j]
            m_new = jnp.maximum(m_prev, x_scr[j])
            alpha = jnp.exp2(m_prev - m_new)
            part = None
            for c0 in range(0, tk, ATTN_CHUNK):
                p = jnp.exp2(s_scr[j, c0:c0 + ATTN_CHUNK, :] - m_new)
                p_scr[j, c0:c0 + ATTN_CHUNK, :] = p.astype(ACT)
                part = p if part is None else part + p
            l_scr[j] = alpha * l_scr[j] + jnp.sum(part, axis=0, keepdims=True)
            m_scr[j] = m_new
            a_scr[j] = alpha

    def values(t, p_scr, a_scr):
        vt = vtl_scr[t]
        for j in range(2):
            acc_scr[j] = a_scr[j] * acc_scr[j] + _dot(vt, p_scr[j])

    @pl.when(qi == 0)
    def _():
        vtc_scr[...] = v_ref[0:n_ctx, :].astype(F32).T.astype(ACT)

        def fill(t, carry):
            rows = pl.ds(pl.multiple_of(n_ctx + t * tk, n_ctx), tk)
            vtl_scr[t] = v_ref[rows, :].astype(F32).T.astype(ACT)
            return carry
        lax.fori_loop(0, n_lat_steps, fill, 0)
        context_keys()
        finish()

    @pl.when(qi > 0)
    def _():
        context_keys()
        scores(0, s0_scr, x0_scr)
        scores(1, s1_scr, x1_scr)
        softmax(s0_scr, x0_scr, p0_scr, a0_scr)

        def body(i, carry):
            t = 2 * i
            scores(t + 2, s0_scr, x0_scr)
            softmax(s1_scr, x1_scr, p1_scr, a1_scr)
            values(t, p0_scr, a0_scr)
            scores(t + 3, s1_scr, x1_scr)
            softmax(s0_scr, x0_scr, p0_scr, a0_scr)
            values(t + 1, p1_scr, a1_scr)
            return carry
        lax.fori_loop(0, n_lat_steps // 2 - 1, body, 0, unroll=True)
        softmax(s1_scr, x1_scr, p1_scr, a1_scr)
        values(n_lat_steps - 2, p0_scr, a0_scr)
        values(n_lat_steps - 1, p1_scr, a1_scr)
        finish()


def _diff_attention(qkv, lq1, lk1, lq2, lk2, subln_g, *, B, L, n_ctx, lambda_init):
    M, N = qkv.shape
    width = 2 * DIFF_DK
    H = N // (3 * width)
    tq = n_ctx
    nq = L // tq
    tk = _row_tile(L - n_ctx, n_ctx, 512)
    n_steps = (L - n_ctx) // tk
    assert n_steps >= 2 and n_steps % 2 == 0
    stage = lambda dtype: pltpu.VMEM((2, tk, tq), dtype)
    stat = pltpu.VMEM((2, 1, tq), F32)
    fixed = lambda b, h, i: (0, 0)
    vec = pl.BlockSpec((1, DIFF_DK), fixed)
    return pl.pallas_call(
        functools.partial(_attn_kernel, n_ctx=n_ctx, tk=tk, lambda_init=lambda_init),
        grid=(B, H, nq),
        in_specs=[pl.BlockSpec((tq, width), lambda b, h, i: (b * nq + i, h)),
                  pl.BlockSpec((L, width), lambda b, h, i: (b, H + h)),
                  pl.BlockSpec((L, DIFF_DV), lambda b, h, i: (b, 2 * H + h)),
                  vec, vec, vec, vec,
                  pl.BlockSpec((DIFF_DV, 1), fixed)],
        out_specs=pl.BlockSpec((tq, DIFF_DV), lambda b, h, i: (b * nq + i, h)),
        out_shape=jax.ShapeDtypeStruct((M, H * DIFF_DV), ACT),
        scratch_shapes=[pltpu.VMEM((DIFF_DV, n_ctx), ACT),
                        pltpu.VMEM((n_steps, DIFF_DV, tk), ACT),
                        pltpu.VMEM((2, DIFF_DV, tq), F32), stat, stat,
                        stage(F32), stage(F32), stage(ACT), stage(ACT), stat, stat, stat, stat],
        compiler_params=_params(3),
        name="diff_attention",
    )(qkv, qkv, qkv, lq1.reshape(1, -1), lk1.reshape(1, -1), lq2.reshape(1, -1),
      lk2.reshape(1, -1), subln_g.reshape(-1, 1))


def _rope_layout():
    quarter = DIFF_DK // 4
    idx = jnp.arange(DIFF_DK).reshape(4, quarter)
    return idx[jnp.array([0, 2, 1, 3])].reshape(-1)


def _rope_tables(n_ctx, n_lat):
    axis_dim = DIFF_DK // 2
    rows = n_lat // GRID_W
    pos_row = jnp.repeat(jnp.arange(rows, dtype=F32), GRID_W)
    pos_col = jnp.tile(jnp.arange(GRID_W, dtype=F32), rows)
    inv_freq = ROPE_BASE ** (-jnp.arange(0, axis_dim, 2, dtype=F32) / axis_dim)
    ang_r = pos_row[:, None] * inv_freq
    ang_c = pos_col[:, None] * inv_freq
    ang = jnp.concatenate([ang_r, ang_r, ang_c, ang_c], axis=-1)
    cos = jnp.concatenate([jnp.ones((n_ctx, DIFF_DK), F32), jnp.cos(ang)], axis=0)
    sin = jnp.concatenate([jnp.zeros((n_ctx, DIFF_DK), F32), jnp.sin(ang)], axis=0)
    first_half = (jnp.arange(DIFF_DK) % (axis_dim)) < axis_dim // 2
    order = _rope_layout()
    return cos[:, order], jnp.where(first_half, -sin, sin)[:, order]


def _final_kernel(h_ref, g_ref, o_ref):
    o_ref[...] = _rms(h_ref[...]) * g_ref[...]


def _final_norm(hs, g, *, B, L, n_ctx):
    D = hs.shape[1]
    tm = n_ctx
    skip = n_ctx // tm
    n_lat = L - n_ctx
    return pl.pallas_call(
        _final_kernel,
        grid=(B, n_lat // tm),
        in_specs=[pl.BlockSpec((None, tm, D), lambda b, i: (b, i + skip, 0)),
                  pl.BlockSpec((1, D), lambda b, i: (0, 0))],
        out_specs=pl.BlockSpec((None, tm, D), lambda b, i: (b, i, 0)),
        out_shape=jax.ShapeDtypeStruct((B, n_lat, D), F32),
        compiler_params=_params(2),
        name="final_norm",
    )(hs.reshape(B, L, D), g.reshape(1, D))


def kernel(x, c, ctx, c_ctx, mod_w, mod_b, norm_mix_g, norm_mlp_g, mlp_w1, mlp_w2, gla_w_in, gla_wa1_f, gla_wa2_f, gla_ba_f, gla_wa1_b, gla_wa2_b, gla_ba_b, gla_norm_g, gla_w_o, conv_w1, conv_b1, conv_dw, conv_dwb, conv_ln_g, conv_ln_b, conv_w2, conv_b2, diff_w_qkv, diff_lq1, diff_lk1, diff_lq2, diff_lk2, diff_subln_g, diff_w_o, final_g):
    B, n_lat, D = x.shape
    n_ctx = ctx.shape[1]
    L = n_ctx + n_lat
    depth = mod_w.shape[0]
    assert B + 1 <= MOD_ROWS and n_lat % n_ctx == 0 and n_ctx % GLA_CHUNK == 0
    dims = dict(L=L, n_ctx=n_ctx)

    hs = jnp.concatenate([ctx, x], axis=1).reshape(B * L, D)
    cvec = jnp.zeros((MOD_ROWS, D), F32).at[:B].set(c).at[B].set(c_ctx)
    mods = _mod_vectors(cvec, mod_w, mod_b)
    rope = _rope_tables(n_ctx, n_lat)
    w1_all, w2_all = mlp_w1.astype(ACT), mlp_w2.astype(ACT)

    for i in range(depth):
        ml = mods[i, :B][:, None, :]
        mc = mods[i, B:B + 1]
        kind, j = i % 3, i // 3
        if kind == 0:
            hk = gla_wa2_f.shape[2]
            dk, dv = hk // GLA_HEADS, gla_w_o.shape[1] // GLA_HEADS
            wa1 = jnp.zeros((D, LANE), F32).at[:, :GLA_RANK].set(gla_wa1_f[j])
            wa1 = wa1.at[:, GLA_RANK:2 * GLA_RANK].set(gla_wa1_b[j])
            wa2 = jnp.zeros((2, LANE, hk), F32).at[0, :GLA_RANK].set(gla_wa2_f[j])
            wa2 = wa2.at[1, GLA_RANK:2 * GLA_RANK].set(gla_wa2_b[j])
            ba = jnp.stack([gla_ba_f[j], gla_ba_b[j]])[:, None, :]
            qkvg, t = _project(hs, norm_mix_g[i], ml, mc, gla_w_in[j].astype(ACT), mode="gla",
                               wa1=wa1.astype(ACT), **dims)
            o_f, o_b = _gla_scan(qkvg, t, wa2.astype(ACT), ba, B=B, dk=dk, dv=dv, **dims)
            hs = _out_residual(hs, ml, mc, gla_w_o[j].astype(ACT),
                               gla=(o_f, o_b, qkvg, gla_norm_g[j]), **dims)
        elif kind == 1:
            glu = _project(hs, norm_mix_g[i], ml, mc, conv_w1[j].astype(ACT), mode="glu",
                           bias=conv_b1[j], **dims)
            hs = _conv_out(hs, ml, mc, glu, conv_dw[j], conv_dwb[j], conv_ln_g[j], conv_ln_b[j],
                           conv_w2[j].astype(ACT), conv_b2[j], **dims)
        else:
            lambda_init = 0.8 - 0.6 * math.exp(-0.3 * i)
            n_qk = diff_w_qkv.shape[2] // 3
            w_qk = diff_w_qkv[j][:, :2 * n_qk].reshape(D, -1, DIFF_DK)[:, :, _rope_layout()]
            w_qkv = jnp.concatenate([w_qk.reshape(D, 2 * n_qk), diff_w_qkv[j][:, 2 * n_qk:]], axis=1)
            qkv = _project(hs, norm_mix_g[i], ml, mc, w_qkv.astype(ACT), mode="rope",
                           rope=rope, n_rope_cols=2 * n_qk, n_q_cols=n_qk,
                           q_scale=DIFF_DK ** -0.5 * math.log2(math.e), **dims)
            att = _diff_attention(qkv, diff_lq1[j], diff_lk1[j], diff_lq2[j], diff_lk2[j],
                                  diff_subln_g[j], B=B, lambda_init=lambda_init, **dims)
            hs = _out_residual(hs, ml, mc, diff_w_o[j].astype(ACT), a=att, **dims)
        hs = _mlp(hs, norm_mlp_g[i], ml, mc, w1_all, w2_all, i, **dims)
    return _final_norm(hs, final_g, B=B, L=L, n_ctx=n_ctx)
```

```python
import functools
import math

import jax
import jax.numpy as jnp
from jax import lax
from jax.experimental import pallas as pl
from jax.experimental.pallas import tpu as pltpu

F32 = jnp.float32
ACT = jnp.bfloat16
EPS = 1e-6
N_MOD = 6
GLA_HEADS = 4
GLA_RANK = 16
GLA_GATE_NORM = 16.0
GLA_CHUNK = 64
CONV_WIDTH = 31
CONV_HALO = 16
DIFF_DK = 128
DIFF_DV = 256
GRID_W = 64
ROPE_BASE = 10000.0
LANE = 128
SUBLANE = 8
MOD_ROWS = 8
VMEM_LIMIT = 56 * 2 ** 20
ATTN_CHUNK = 256

def _params(n_axes):
    return pltpu.CompilerParams(dimension_semantics=("arbitrary",) * n_axes,
                                vmem_limit_bytes=VMEM_LIMIT)


def _dot(a, b):
    return jnp.dot(a, b, preferred_element_type=F32)


def _dot_nt(a, b):
    return lax.dot_general(a, b, (((1,), (1,)), ((), ())), preferred_element_type=F32)


def _dot_tn(a, b):
    return lax.dot_general(a, b, (((0,), (0,)), ((), ())), preferred_element_type=F32)


def _sigmoid(x):
    return 1.0 / (1.0 + jnp.exp(-x))


def _rms(x):
    return x * lax.rsqrt(jnp.mean(x * x, axis=-1, keepdims=True) + EPS)


def _row_tile(L, n_ctx, cap):
    best = n_ctx
    for m in range(1, L // n_ctx + 1):
        t = m * n_ctx
        if t <= cap and L % t == 0:
            best = t
    return best


def _col_tile(n, cap):
    best = LANE
    t = LANE
    while t <= min(n, cap):
        if n % t == 0:
            best = t
        t += LANE
    return best


def _mod_pick(ml_ref, mc_ref, k, D, ctx_row):
    lat = ml_ref[:, k * D:(k + 1) * D]
    ctx = mc_ref[:, k * D:(k + 1) * D]
    return jnp.where(ctx_row, ctx, lat)


def _ctx_rows(tm, n_ctx, first_tile):
    rows = lax.broadcasted_iota(jnp.int32, (tm, 1), 0)
    return jnp.logical_and(first_tile, rows < n_ctx)


def _norm_mod_store(h_ref, g_ref, ml_ref, mc_ref, k_shift, k_scale, first_tile, n_ctx, store):
    h = h_ref[...]
    tm, D = h.shape
    y = _rms(h)
    g = g_ref[...]
    lat_a = g * (1.0 + ml_ref[:, k_scale * D:(k_scale + 1) * D])
    lat_b = ml_ref[:, k_shift * D:(k_shift + 1) * D]

    @pl.when(jnp.logical_not(first_tile))
    def _():
        store(y * lat_a + lat_b)

    @pl.when(first_tile)
    def _():
        ctx_row = lax.broadcasted_iota(jnp.int32, (tm, 1), 0) < n_ctx
        ctx_a = g * (1.0 + mc_ref[:, k_scale * D:(k_scale + 1) * D])
        ctx_b = mc_ref[:, k_shift * D:(k_shift + 1) * D]
        store(y * jnp.where(ctx_row, ctx_a, lat_a) + jnp.where(ctx_row, ctx_b, lat_b))


def _mod_kernel(c_ref, w_ref, b_ref, o_ref):
    c = c_ref[...]
    sc = (c * _sigmoid(c)).astype(ACT)
    o_ref[...] = _dot(sc, w_ref[...].astype(ACT)) + b_ref[...]


def _mod_vectors(cvec, mod_w, mod_b):
    depth, D, N = mod_w.shape
    tn = _col_tile(N, 1024)
    return pl.pallas_call(
        _mod_kernel,
        grid=(depth, N // tn),
        in_specs=[pl.BlockSpec((MOD_ROWS, D), lambda l, j: (0, 0)),
                  pl.BlockSpec((None, D, tn), lambda l, j: (l, 0, j)),
                  pl.BlockSpec((None, 1, tn), lambda l, j: (l, 0, j))],
        out_specs=pl.BlockSpec((None, MOD_ROWS, tn), lambda l, j: (l, 0, j)),
        out_shape=jax.ShapeDtypeStruct((depth, MOD_ROWS, N), F32),
        compiler_params=_params(2),
        name="mod_vectors",
    )(cvec, mod_w, mod_b.reshape(depth, 1, N))


def _rope_tile(acc, cos, sin):
    return acc * cos + pltpu.roll(acc, DIFF_DK // 2, axis=1) * sin


def _proj_kernel(*refs, mode, tpb, n_ctx, n_rope_tiles, q_tiles, q_scale):
    if mode == "glu":
        h_ref, g_ref, ml_ref, mc_ref, wa_ref, wg_ref, ba_ref, bg_ref, o_ref, u_scr = refs
    elif mode == "rope":
        h_ref, g_ref, ml_ref, mc_ref, w_ref, cos_ref, sin_ref, o_ref, u_scr = refs
    else:
        h_ref, g_ref, ml_ref, mc_ref, w_ref, wa1_ref, o_ref, t_ref, u_scr = refs
    i = pl.program_id(0)
    j = pl.program_id(1)

    @pl.when(j == 0)
    def _():
        def store(u):
            u_scr[...] = u.astype(ACT)
        _norm_mod_store(h_ref, g_ref, ml_ref, mc_ref, 0, 1, i % tpb == 0, n_ctx, store)
        if mode == "gla":
            t_ref[...] = _dot(u_scr[...], wa1_ref[...]).astype(t_ref.dtype)

    u = u_scr[...]
    if mode == "glu":
        a = _dot(u, wa_ref[...]) + ba_ref[...]
        g = _dot(u, wg_ref[...]) + bg_ref[...]
        o_ref[...] = (a * _sigmoid(g)).astype(o_ref.dtype)
    elif mode == "rope":
        acc = _dot(u, w_ref[...])

        @pl.when(j < n_rope_tiles)
        def _():
            scale = jnp.where(j < q_tiles, q_scale, 1.0)
            cos, sin = cos_ref[...], sin_ref[...]
            for s in range(acc.shape[1] // LANE):
                seg = acc[:, s * LANE:(s + 1) * LANE]
                o_ref[:, s * LANE:(s + 1) * LANE] = (
                    _rope_tile(seg, cos, sin) * scale).astype(o_ref.dtype)

        @pl.when(j >= n_rope_tiles)
        def _():
            o_ref[...] = acc.astype(o_ref.dtype)
    else:
        o_ref[...] = _dot(u, w_ref[...]).astype(o_ref.dtype)


def _project(hs, g, ml, mc, w, *, L, n_ctx, mode, bias=None, rope=None, wa1=None,
             n_rope_cols=0, n_q_cols=0, q_scale=1.0):
    M, D = hs.shape
    N = w.shape[1]
    n_out = N // 2 if mode == "glu" else N
    tm = _row_tile(L, n_ctx, 768)
    tpb = L // tm
    tn = _col_tile(n_out, 1024)
    if mode == "rope":
        while n_rope_cols % tn or n_q_cols % tn:
            tn -= LANE
    nj = n_out // tn
    row = lambda i, j: (i, 0)
    fixed = lambda i, j: (0, 0)
    in_specs = [pl.BlockSpec((tm, D), row),
                pl.BlockSpec((1, D), fixed),
                pl.BlockSpec((None, 1, N_MOD * D), lambda i, j: (i // tpb, 0, 0)),
                pl.BlockSpec((1, N_MOD * D), fixed)]
    args = [hs, g.reshape(1, D), ml, mc]
    if mode == "glu":
        in_specs += [pl.BlockSpec((D, tn), lambda i, j: (0, j)),
                     pl.BlockSpec((D, tn), lambda i, j: (0, j + nj)),
                     pl.BlockSpec((1, tn), lambda i, j: (0, j)),
                     pl.BlockSpec((1, tn), lambda i, j: (0, j + nj))]
        b2 = bias.reshape(1, N)
        args += [w, w, b2, b2]
    else:
        in_specs.append(pl.BlockSpec((D, tn), lambda i, j: (0, j)))
        args.append(w)
        if mode == "rope":
            tab = pl.BlockSpec((tm, LANE), lambda i, j: (i % tpb, 0))
            in_specs += [tab, tab]
            args += list(rope)
    out_specs = pl.BlockSpec((tm, tn), lambda i, j: (i, j))
    out_shape = jax.ShapeDtypeStruct((M, n_out), ACT)
    if mode == "gla":
        in_specs.append(pl.BlockSpec((D, LANE), fixed))
        args.append(wa1)
        out_specs = [out_specs, pl.BlockSpec((tm, LANE), row)]
        out_shape = [out_shape, jax.ShapeDtypeStruct((M, LANE), ACT)]
    kern = functools.partial(_proj_kernel, mode=mode, tpb=tpb, n_ctx=n_ctx,
                             n_rope_tiles=n_rope_cols // tn, q_tiles=n_q_cols // tn,
                             q_scale=q_scale)
    return pl.pallas_call(
        kern,
        grid=(M // tm, nj),
        in_specs=in_specs,
        out_specs=out_specs,
        out_shape=out_shape,
        scratch_shapes=[pltpu.VMEM((tm, D), ACT)],
        compiler_params=_params(2),
        name="project_" + mode,
    )(*args)


def _mlp_kernel(h_ref, g_ref, ml_ref, mc_ref, w1_ref, w2_ref, o_ref, u_scr, *, tpb, n_ctx):
    i = pl.program_id(0)
    j = pl.program_id(1)
    tm, D = h_ref.shape
    first_tile = i % tpb == 0

    @pl.when(j == 0)
    def _():
        def store(u):
            u_scr[...] = u.astype(ACT)
        _norm_mod_store(h_ref, g_ref, ml_ref, mc_ref, 3, 4, first_tile, n_ctx, store)
        o_ref[...] = jnp.zeros_like(o_ref)

    hid = jnp.maximum(_dot(u_scr[...], w1_ref[...]), 0.0)
    o_ref[...] += _dot((hid * hid).astype(ACT), w2_ref[...])

    @pl.when(j == pl.num_programs(1) - 1)
    def _():
        gate = _mod_pick(ml_ref, mc_ref, 5, D, _ctx_rows(tm, n_ctx, first_tile))
        o_ref[...] = h_ref[...] + gate * o_ref[...]


def _mlp(hs, g, ml, mc, w1, w2, layer, *, L, n_ctx):
    M, D = hs.shape
    Hd = w1.shape[2]
    tm = _row_tile(L, n_ctx, 768)
    tpb = L // tm
    th = _col_tile(Hd, 1024)
    row = lambda i, j: (i, 0)
    fixed = lambda i, j: (0, 0)
    return pl.pallas_call(
        functools.partial(_mlp_kernel, tpb=tpb, n_ctx=n_ctx),
        grid=(M // tm, Hd // th),
        in_specs=[pl.BlockSpec((tm, D), row),
                  pl.BlockSpec((1, D), fixed),
                  pl.BlockSpec((None, 1, N_MOD * D), lambda i, j: (i // tpb, 0, 0)),
                  pl.BlockSpec((1, N_MOD * D), fixed),
                  pl.BlockSpec((None, D, th), lambda i, j: (layer, 0, j)),
                  pl.BlockSpec((None, th, D), lambda i, j: (layer, j, 0))],
        out_specs=pl.BlockSpec((tm, D), row),
        out_shape=jax.ShapeDtypeStruct((M, D), F32),
        scratch_shapes=[pltpu.VMEM((tm, D), ACT)],
        compiler_params=_params(2),
        name="mlp",
    )(hs, g.reshape(1, D), ml, mc, w1, w2)


def _out_kernel(*refs, mode, tpb, n_ctx, n_heads):
    if mode == "gla":
        h_ref, ml_ref, mc_ref, w_ref, of_ref, ob_ref, gt_ref, ng_ref, o_ref = refs
    else:
        h_ref, ml_ref, mc_ref, w_ref, a_ref, o_ref = refs
    i = pl.program_id(0)
    tm, D = h_ref.shape
    ctx_row = _ctx_rows(tm, n_ctx, i % tpb == 0)
    if mode == "gla":
        o = of_ref[...].astype(F32) + ob_ref[...].astype(F32)
        dv = o.shape[1] // n_heads
        ng = ng_ref[...]
        o = jnp.concatenate([_rms(o[:, hh * dv:(hh + 1) * dv]) * ng for hh in range(n_heads)],
                            axis=1)
        gt = gt_ref[...].astype(F32)
        a = (o * (gt * _sigmoid(gt))).astype(ACT)
    else:
        a = a_ref[...]
    gate = _mod_pick(ml_ref, mc_ref, 2, D, ctx_row)
    o_ref[...] = h_ref[...] + gate * _dot(a, w_ref[...])


def _out_residual(hs, ml, mc, w, *, L, n_ctx, a=None, gla=None):
    M, D = hs.shape
    K = w.shape[0]
    tm = _row_tile(L, n_ctx, 256)
    tpb = L // tm
    row = lambda i: (i, 0)
    fixed = lambda i: (0, 0)
    in_specs = [pl.BlockSpec((tm, D), row),
                pl.BlockSpec((None, 1, N_MOD * D), lambda i: (i // tpb, 0, 0)),
                pl.BlockSpec((1, N_MOD * D), fixed),
                pl.BlockSpec((K, D), fixed)]
    args = [hs, ml, mc, w]
    if gla is not None:
        o_f, o_b, qkvg, norm_g = gla
        g_blk = qkvg.shape[1] // K - 1
        in_specs += [pl.BlockSpec((tm, K), row), pl.BlockSpec((tm, K), row),
                     pl.BlockSpec((tm, K), lambda i: (i, g_blk)),
                     pl.BlockSpec((1, norm_g.shape[0]), fixed)]
        args += [o_f, o_b, qkvg, norm_g.reshape(1, -1)]
        mode = "gla"
    else:
        in_specs.append(pl.BlockSpec((tm, K), row))
        args.append(a)
        mode = "plain"
    return pl.pallas_call(
        functools.partial(_out_kernel, mode=mode, tpb=tpb, n_ctx=n_ctx, n_heads=GLA_HEADS),
        grid=(M // tm,),
        in_specs=in_specs,
        out_specs=pl.BlockSpec((tm, D), row),
        out_shape=jax.ShapeDtypeStruct((M, D), F32),
        compiler_params=_params(1),
        name="out_" + mode,
    )(*args)


def _gla_direction(t_ref, wa2_ref, ba_ref, q_ref, k_ref, v_ref, o_ref, s_scr,
                   la_scr, qd_scr, qa_scr, qf_scr, kr_scr, km_scr, oi_scr, af_scr,
                   *, forward, q_scale):
    C = GLA_CHUNK
    blk = q_ref.shape[0]
    n = blk // C
    dk, dv = s_scr.shape
    order = list(range(n)) if forward else list(range(n - 1, -1, -1))
    last_row = C - 1 if forward else 0
    r = lax.broadcasted_iota(jnp.int32, (blk, blk), 0)
    c = lax.broadcasted_iota(jnp.int32, (blk, blk), 1)
    causal = (c <= r) if forward else (c >= r)
    shift = C.bit_length() - 1
    same_chunk = jnp.right_shift(r, shift) == jnp.right_shift(c, shift)
    tri = jnp.logical_and(causal, same_chunk).astype(ACT)

    z = _dot(t_ref[...], wa2_ref[...]) + ba_ref[...]
    g = (jnp.minimum(z, 0.0) - jnp.log(1.0 + jnp.exp(-jnp.abs(z)))) / GLA_GATE_NORM
    g_hi = g.astype(ACT)
    g_lo = (g - g_hi.astype(F32)).astype(ACT)
    la_scr[...] = _dot(tri, g_hi) + _dot(tri, g_lo)

    totals = []
    for ci in range(n):
        rows = pl.ds(ci * C, C)
        b = la_scr[rows, :]
        total = b[last_row:last_row + 1, :]
        totals.append(total)
        q = q_ref[rows, :].astype(F32) * q_scale
        k = k_ref[rows, :].astype(F32)
        q_dec = q * jnp.exp(b)
        qd_scr[rows, :] = q_dec
        qa_scr[rows, :] = q_dec.astype(ACT)
        km_scr[rows, :] = (k * jnp.exp(-b)).astype(ACT)
        kr_scr[rows, :] = k * jnp.exp(total - b)

    run = None
    for ci in order:
        rows = pl.ds(ci * C, C)
        q_dec = qd_scr[rows, :]
        qf_scr[rows, :] = (q_dec if run is None else q_dec * jnp.exp(run)).astype(ACT)
        run = totals[ci] if run is None else run + totals[ci]
    state = s_scr[...]
    oi_scr[...] = _dot(qf_scr[...], state.astype(ACT))

    def keys_at(pos, cj):
        gap = None
        for cm in order[order.index(cj) + 1:pos]:
            gap = totals[cm] if gap is None else gap + totals[cm]
        k_end = kr_scr[pl.ds(cj * C, C), :]
        return k_end if gap is None else k_end * jnp.exp(gap)

    for pos, ci in enumerate(order):
        rows = pl.ds(ci * C, C)
        keys = jnp.concatenate(
            [keys_at(pos, cj).astype(ACT) if cj in order[:pos] else km_scr[pl.ds(cj * C, C), :]
             for cj in range(n)], axis=0)
        att = _dot_nt(qa_scr[rows, :], keys)
        af_scr[rows, :] = jnp.where(causal[ci * C:(ci + 1) * C, :], att, 0.0).astype(ACT)

    k_fin = jnp.concatenate([keys_at(n, cj) for cj in range(n)], axis=0)
    both = _dot(jnp.concatenate([af_scr[...], k_fin.T.astype(ACT)], axis=0), v_ref[...])
    o_ref[...] = (both[:blk, :] + oi_scr[...]).astype(o_ref.dtype)
    decay_col = jnp.broadcast_to(jnp.exp(run), (LANE, dk)).T
    s_scr[...] = state * jnp.tile(decay_col, (1, dv // LANE)) + both[blk:, :]


def _gla_kernel(tf_ref, w2f_ref, baf_ref, qf_ref, kf_ref, vf_ref,
                tb_ref, w2b_ref, bab_ref, qb_ref, kb_ref, vb_ref,
                of_ref, ob_ref, sf_scr, sb_scr, *scratch, q_scale):
    @pl.when(pl.program_id(2) == 0)
    def _():
        sf_scr[...] = jnp.zeros_like(sf_scr)
        sb_scr[...] = jnp.zeros_like(sb_scr)

    half = len(scratch) // 2
    _gla_direction(tf_ref, w2f_ref, baf_ref, qf_ref, kf_ref, vf_ref, of_ref, sf_scr,
                   *scratch[:half], forward=True, q_scale=q_scale)
    _gla_direction(tb_ref, w2b_ref, bab_ref, qb_ref, kb_ref, vb_ref, ob_ref, sb_scr,
                   *scratch[half:], forward=False, q_scale=q_scale)


def _gla_scan(qkvg, t, wa2, ba, *, B, L, n_ctx, dk, dv):
    M = qkvg.shape[0]
    H = GLA_HEADS
    blk = n_ctx
    nb = L // blk
    kq = (H * dk) // dk
    kv = (2 * H * dk) // dv

    def fwd(n):
        return n

    def bwd(n):
        return jnp.where(n == 0, 0, nb - n)

    def direction_specs(d, order):
        rows = lambda b, n: b * nb + order(n)
        return [pl.BlockSpec((blk, LANE), lambda b, h, n: (rows(b, n), 0)),
                pl.BlockSpec((None, LANE, dk), lambda b, h, n: (d, 0, h)),
                pl.BlockSpec((None, 1, dk), lambda b, h, n: (d, 0, h)),
                pl.BlockSpec((blk, dk), lambda b, h, n: (rows(b, n), h)),
                pl.BlockSpec((blk, dk), lambda b, h, n: (rows(b, n), kq + h)),
                pl.BlockSpec((blk, dv), lambda b, h, n: (rows(b, n), kv + h))]

    out_spec = lambda order: pl.BlockSpec((blk, dv), lambda b, h, n: (b * nb + order(n), h))
    stage = [pltpu.VMEM((blk, dk), F32), pltpu.VMEM((blk, dk), F32), pltpu.VMEM((blk, dk), ACT),
             pltpu.VMEM((blk, dk), ACT), pltpu.VMEM((blk, dk), F32), pltpu.VMEM((blk, dk), ACT),
             pltpu.VMEM((blk, dv), F32), pltpu.VMEM((blk, blk), ACT)]
    return pl.pallas_call(
        functools.partial(_gla_kernel, q_scale=dk ** -0.5),
        grid=(B, H, nb),
        in_specs=direction_specs(0, fwd) + direction_specs(1, bwd),
        out_specs=[out_spec(fwd), out_spec(bwd)],
        out_shape=[jax.ShapeDtypeStruct((M, H * dv), ACT)] * 2,
        scratch_shapes=[pltpu.VMEM((dk, dv), F32), pltpu.VMEM((dk, dv), F32)] + stage + stage,
        compiler_params=_params(3),
        name="gla_scan",
    )(t, wa2, ba, qkvg, qkvg, qkvg, t, wa2, ba, qkvg, qkvg, qkvg)


def _conv_kernel(h_ref, ml_ref, mc_ref, x_ref, prev_ref, next_ref, dw_ref, dwb_ref,
                 lng_ref, lnb_ref, w_ref, b_ref, o_ref, xw_scr, y_scr, *, tpb):
    i = pl.program_id(0)
    tm, D = h_ref.shape
    t = i % tpb
    has_prev = t > 1
    has_next = jnp.logical_and(t > 0, t < tpb - 1)
    xw_scr[0, 0:CONV_HALO, :] = jnp.where(has_prev, prev_ref[...].astype(F32), 0.0)
    xw_scr[0, CONV_HALO:CONV_HALO + tm, :] = x_ref[...].astype(F32)
    xw_scr[0, CONV_HALO + tm:, :] = jnp.where(has_next, next_ref[...].astype(F32), 0.0)
    n_shift = tm + 2 * CONV_HALO - SUBLANE
    for r in range(1, SUBLANE):
        xw_scr[r, 0:n_shift, :] = xw_scr[0, r:r + n_shift, :]
    first = CONV_HALO - CONV_WIDTH // 2
    rb = min(tm, 128)

    def col_body(cb, carry):
        cols = pl.ds(pl.multiple_of(cb * LANE, LANE), LANE)
        for r0 in range(0, tm, rb):
            acc = jnp.zeros((rb, LANE), F32)
            for k in range(CONV_WIDTH):
                a, r = divmod(first + k, SUBLANE)
                tap = xw_scr[r, pl.ds(r0 + a * SUBLANE, rb), cols]
                acc = acc + tap * dw_ref[pl.ds(k, 1), cols]
            y_scr[pl.ds(r0, rb), cols] = acc + dwb_ref[:, cols]
        return carry

    lax.fori_loop(0, D // LANE, col_body, 0)
    y = y_scr[...]
    yc = y - jnp.mean(y, axis=-1, keepdims=True)
    yn = yc * lax.rsqrt(jnp.mean(yc * yc, axis=-1, keepdims=True) + EPS)
    yn = yn * lng_ref[...] + lnb_ref[...]
    a = (yn * _sigmoid(yn)).astype(ACT)
    gate = _mod_pick(ml_ref, mc_ref, 2, D, t == 0)
    o_ref[...] = h_ref[...] + gate * (_dot(a, w_ref[...]) + b_ref[...])


def _conv_out(hs, ml, mc, glu, dw, dwb, ln_g, ln_b, w2, b2, *, L, n_ctx):
    M, D = hs.shape
    tm = n_ctx
    tpb = L // tm
    hb = tm // CONV_HALO
    n_halo = M // CONV_HALO
    row = lambda i: (i, 0)
    fixed = lambda i: (0, 0)
    vec = pl.BlockSpec((1, D), fixed)
    return pl.pallas_call(
        functools.partial(_conv_kernel, tpb=tpb),
        grid=(M // tm,),
        in_specs=[pl.BlockSpec((tm, D), row),
                  pl.BlockSpec((None, 1, N_MOD * D), lambda i: (i // tpb, 0, 0)),
                  pl.BlockSpec((1, N_MOD * D), fixed),
                  pl.BlockSpec((tm, D), row),
                  pl.BlockSpec((CONV_HALO, D), lambda i: (jnp.maximum(i * hb - 1, 0), 0)),
                  pl.BlockSpec((CONV_HALO, D), lambda i: (jnp.minimum((i + 1) * hb, n_halo - 1), 0)),
                  pl.BlockSpec((CONV_WIDTH, D), fixed),
                  vec, vec, vec,
                  pl.BlockSpec((D, D), fixed),
                  vec],
        out_specs=pl.BlockSpec((tm, D), row),
        out_shape=jax.ShapeDtypeStruct((M, D), F32),
        scratch_shapes=[pltpu.VMEM((SUBLANE, tm + 2 * CONV_HALO, D), F32),
                        pltpu.VMEM((tm, D), F32)],
        compiler_params=_params(1),
        name="conv_out",
    )(hs, ml, mc, glu, glu, glu, dw, dwb.reshape(1, D), ln_g.reshape(1, D), ln_b.reshape(1, D),
      w2, b2.reshape(1, D))


def _attn_kernel(q_ref, k_ref, v_ref, lq1_ref, lk1_ref, lq2_ref, lk2_ref, sg_ref, o_ref,
                 vtc_scr, vtl_scr, acc_scr, m_scr, l_scr, s0_scr, s1_scr, p0_scr, p1_scr,
                 a0_scr, a1_scr, x0_scr, x1_scr, *, n_ctx, tk, lambda_init):
    qi = pl.program_id(2)
    n_lat_steps = vtl_scr.shape[0]

    def context_keys():
        vt = vtc_scr[...]
        for j in range(2):
            lanes = slice(j * DIFF_DK, (j + 1) * DIFF_DK)
            st = _dot_nt(k_ref[0:n_ctx, lanes], q_ref[:, lanes])
            m = jnp.max(st, axis=0, keepdims=True)
            p = jnp.exp2(st - m)
            l_scr[j] = jnp.sum(p, axis=0, keepdims=True)
            acc_scr[j] = _dot(vt, p.astype(ACT))
            m_scr[j] = m

    def finish():
        lam = (jnp.exp(jnp.sum(lq1_ref[...] * lk1_ref[...], axis=-1, keepdims=True))
               - jnp.exp(jnp.sum(lq2_ref[...] * lk2_ref[...], axis=-1, keepdims=True))
               + lambda_init)
        ot = acc_scr[0] * (1.0 / l_scr[0]) - lam * (acc_scr[1] * (1.0 / l_scr[1]))
        ot = ot * lax.rsqrt(jnp.mean(ot * ot, axis=0, keepdims=True) + EPS)
        ot = ot * (sg_ref[...] * (1.0 - lambda_init))
        o_ref[...] = ot.T.astype(o_ref.dtype)

    def scores(t, s_scr, x_scr):
        for j in range(2):
            lanes = slice(j * DIFF_DK, (j + 1) * DIFF_DK)
            part = None
            for c0 in range(0, tk, ATTN_CHUNK):
                rows = pl.ds(pl.multiple_of(n_ctx + t * tk + c0, ATTN_CHUNK), ATTN_CHUNK)
                st = _dot_nt(k_ref[rows, lanes], q_ref[:, lanes])
                s_scr[j, c0:c0 + ATTN_CHUNK, :] = st
                part = st if part is None else jnp.maximum(part, st)
            x_scr[j] = jnp.max(part, axis=0, keepdims=True)

    def softmax(s_scr, x_scr, p_scr, a_scr):
        for j in range(2):
            m_prev = m_scr[j]
            m_new = jnp.maximum(m_prev, x_scr[j])
            alpha = jnp.exp2(m_prev - m_new)
            part = None
            for c0 in range(0, tk, ATTN_CHUNK):
                p = jnp.exp2(s_scr[j, c0:c0 + ATTN_CHUNK, :] - m_new)
                p_scr[j, c0:c0 + ATTN_CHUNK, :] = p.astype(ACT)
                part = p if part is None else part + p
            l_scr[j] = alpha * l_scr[j] + jnp.sum(part, axis=0, keepdims=True)
            m_scr[j] = m_new
            a_scr[j] = alpha

    def values(t, p_scr, a_scr):
        vt = vtl_scr[t]
        for j in range(2):
            acc_scr[j] = a_scr[j] * acc_scr[j] + _dot(vt, p_scr[j])

    @pl.when(qi == 0)
    def _():
        vtc_scr[...] = v_ref[0:n_ctx, :].astype(F32).T.astype(ACT)

        def fill(t, carry):
            rows = pl.ds(pl.multiple_of(n_ctx + t * tk, n_ctx), tk)
            vtl_scr[t] = v_ref[rows, :].astype(F32).T.astype(ACT)
            return carry
        lax.fori_loop(0, n_lat_steps, fill, 0)
        context_keys()
        finish()

    @pl.when(qi > 0)
    def _():
        context_keys()
        scores(0, s0_scr, x0_scr)
        scores(1, s1_scr, x1_scr)
        softmax(s0_scr, x0_scr, p0_scr, a0_scr)

        def body(i, carry):
            t = 2 * i
            scores(t + 2, s0_scr, x0_scr)
            softmax(s1_scr, x1_scr, p1_scr, a1_scr)
            values(t, p0_scr, a0_scr)
            scores(t + 3, s1_scr, x1_scr)
            softmax(s0_scr, x0_scr, p0_scr, a0_scr)
            values(t + 1, p1_scr, a1_scr)
            return carry
        lax.fori_loop(0, n_lat_steps // 2 - 1, body, 0, unroll=True)
        softmax(s1_scr, x1_scr, p1_scr, a1_scr)
        values(n_lat_steps - 2, p0_scr, a0_scr)
        values(n_lat_steps - 1, p1_scr, a1_scr)
        finish()


def _diff_attention(qkv, lq1, lk1, lq2, lk2, subln_g, *, B, L, n_ctx, lambda_init):
    M, N = qkv.shape
    width = 2 * DIFF_DK
    H = N // (3 * width)
    tq = n_ctx
    nq = L // tq
    tk = _row_tile(L - n_ctx, n_ctx, 512)
    n_steps = (L - n_ctx) // tk
    assert n_steps >= 2 and n_steps % 2 == 0
    stage = lambda dtype: pltpu.VMEM((2, tk, tq), dtype)
    stat = pltpu.VMEM((2, 1, tq), F32)
    fixed = lambda b, h, i: (0, 0)
    vec = pl.BlockSpec((1, DIFF_DK), fixed)
    return pl.pallas_call(
        functools.partial(_attn_kernel, n_ctx=n_ctx, tk=tk, lambda_init=lambda_init),
        grid=(B, H, nq),
        in_specs=[pl.BlockSpec((tq, width), lambda b, h, i: (b * nq + i, h)),
                  pl.BlockSpec((L, width), lambda b, h, i: (b, H + h)),
                  pl.BlockSpec((L, DIFF_DV), lambda b, h, i: (b, 2 * H + h)),
                  vec, vec, vec, vec,
                  pl.BlockSpec((DIFF_DV, 1), fixed)],
        out_specs=pl.BlockSpec((tq, DIFF_DV), lambda b, h, i: (b * nq + i, h)),
        out_shape=jax.ShapeDtypeStruct((M, H * DIFF_DV), ACT),
        scratch_shapes=[pltpu.VMEM((DIFF_DV, n_ctx), ACT),
                        pltpu.VMEM((n_steps, DIFF_DV, tk), ACT),
                        pltpu.VMEM((2, DIFF_DV, tq), F32), stat, stat,
                        stage(F32), stage(F32), stage(ACT), stage(ACT), stat, stat, stat, stat],
        compiler_params=_params(3),
        name="diff_attention",
    )(qkv, qkv, qkv, lq1.reshape(1, -1), lk1.reshape(1, -1), lq2.reshape(1, -1),
      lk2.reshape(1, -1), subln_g.reshape(-1, 1))


def _rope_layout():
    quarter = DIFF_DK // 4
    idx = jnp.arange(DIFF_DK).reshape(4, quarter)
    return idx[jnp.array([0, 2, 1, 3])].reshape(-1)


def _rope_tables(n_ctx, n_lat):
    axis_dim = DIFF_DK // 2
    rows = n_lat // GRID_W
    pos_row = jnp.repeat(jnp.arange(rows, dtype=F32), GRID_W)
    pos_col = jnp.tile(jnp.arange(GRID_W, dtype=F32), rows)
    inv_freq = ROPE_BASE ** (-jnp.arange(0, axis_dim, 2, dtype=F32) / axis_dim)
    ang_r = pos_row[:, None] * inv_freq
    ang_c = pos_col[:, None] * inv_freq
    ang = jnp.concatenate([ang_r, ang_r, ang_c, ang_c], axis=-1)
    cos = jnp.concatenate([jnp.ones((n_ctx, DIFF_DK), F32), jnp.cos(ang)], axis=0)
    sin = jnp.concatenate([jnp.zeros((n_ctx, DIFF_DK), F32), jnp.sin(ang)], axis=0)
    first_half = (jnp.arange(DIFF_DK) % (axis_dim)) < axis_dim // 2
    order = _rope_layout()
    return cos[:, order], jnp.where(first_half, -sin, sin)[:, order]


def _final_kernel(h_ref, g_ref, o_ref):
    o_ref[...] = _rms(h_ref[...]) * g_ref[...]


def _final_norm(hs, g, *, B, L, n_ctx):
    D = hs.shape[1]
    tm = n_ctx
    skip = n_ctx // tm
    n_lat = L - n_ctx
    return pl.pallas_call(
        _final_kernel,
        grid=(B, n_lat // tm),
        in_specs=[pl.BlockSpec((None, tm, D), lambda b, i: (b, i + skip, 0)),
                  pl.BlockSpec((1, D), lambda b, i: (0, 0))],
        out_specs=pl.BlockSpec((None, tm, D), lambda b, i: (b, i, 0)),
        out_shape=jax.ShapeDtypeStruct((B, n_lat, D), F32),
        compiler_params=_params(2),
        name="final_norm",
    )(hs.reshape(B, L, D), g.reshape(1, D))


def kernel(x, c, ctx, c_ctx, mod_w, mod_b, norm_mix_g, norm_mlp_g, mlp_w1, mlp_w2, gla_w_in, gla_wa1_f, gla_wa2_f, gla_ba_f, gla_wa1_b, gla_wa2_b, gla_ba_b, gla_norm_g, gla_w_o, conv_w1, conv_b1, conv_dw, conv_dwb, conv_ln_g, conv_ln_b, conv_w2, conv_b2, diff_w_qkv, diff_lq1, diff_lk1, diff_lq2, diff_lk2, diff_subln_g, diff_w_o, final_g):
    B, n_lat, D = x.shape
    n_ctx = ctx.shape[1]
    L = n_ctx + n_lat
    depth = mod_w.shape[0]
    assert B + 1 <= MOD_ROWS and n_lat % n_ctx == 0 and n_ctx % GLA_CHUNK == 0
    dims = dict(L=L, n_ctx=n_ctx)

    hs = jnp.concatenate([ctx, x], axis=1).reshape(B * L, D)
    cvec = jnp.zeros((MOD_ROWS, D), F32).at[:B].set(c).at[B].set(c_ctx)
    mods = _mod_vectors(cvec, mod_w, mod_b)
    rope = _rope_tables(n_ctx, n_lat)
    w1_all, w2_all = mlp_w1.astype(ACT), mlp_w2.astype(ACT)

    for i in range(depth):
        ml = mods[i, :B][:, None, :]
        mc = mods[i, B:B + 1]
        kind, j = i % 3, i // 3
        if kind == 0:
            hk = gla_wa2_f.shape[2]
            dk, dv = hk // GLA_HEADS, gla_w_o.shape[1] // GLA_HEADS
            wa1 = jnp.zeros((D, LANE), F32).at[:, :GLA_RANK].set(gla_wa1_f[j])
            wa1 = wa1.at[:, GLA_RANK:2 * GLA_RANK].set(gla_wa1_b[j])
            wa2 = jnp.zeros((2, LANE, hk), F32).at[0, :GLA_RANK].set(gla_wa2_f[j])
            wa2 = wa2.at[1, GLA_RANK:2 * GLA_RANK].set(gla_wa2_b[j])
            ba = jnp.stack([gla_ba_f[j], gla_ba_b[j]])[:, None, :]
            qkvg, t = _project(hs, norm_mix_g[i], ml, mc, gla_w_in[j].astype(ACT), mode="gla",
                               wa1=wa1.astype(ACT), **dims)
            o_f, o_b = _gla_scan(qkvg, t, wa2.astype(ACT), ba, B=B, dk=dk, dv=dv, **dims)
            hs = _out_residual(hs, ml, mc, gla_w_o[j].astype(ACT),
                               gla=(o_f, o_b, qkvg, gla_norm_g[j]), **dims)
        elif kind == 1:
            glu = _project(hs, norm_mix_g[i], ml, mc, conv_w1[j].astype(ACT), mode="glu",
                           bias=conv_b1[j], **dims)
            hs = _conv_out(hs, ml, mc, glu, conv_dw[j], conv_dwb[j], conv_ln_g[j], conv_ln_b[j],
                           conv_w2[j].astype(ACT), conv_b2[j], **dims)
        else:
            lambda_init = 0.8 - 0.6 * math.exp(-0.3 * i)
            n_qk = diff_w_qkv.shape[2] // 3
            w_qk = diff_w_qkv[j][:, :2 * n_qk].reshape(D, -1, DIFF_DK)[:, :, _rope_layout()]
            w_qkv = jnp.concatenate([w_qk.reshape(D, 2 * n_qk), diff_w_qkv[j][:, 2 * n_qk:]], axis=1)
            qkv = _project(hs, norm_mix_g[i], ml, mc, w_qkv.astype(ACT), mode="rope",
                           rope=rope, n_rope_cols=2 * n_qk, n_q_cols=n_qk,
                           q_scale=DIFF_DK ** -0.5 * math.log2(math.e), **dims)
            att = _diff_attention(qkv, diff_lq1[j], diff_lk1[j], diff_lq2[j], diff_lk2[j],
                                  diff_subln_g[j], B=B, lambda_init=lambda_init, **dims)
            hs = _out_residual(hs, ml, mc, diff_w_o[j].astype(ACT), a=att, **dims)
        hs = _mlp(hs, norm_mlp_g[i], ml, mc, w1_all, w2_all, i, **dims)
    return _final_norm(hs, final_g, B=B, L=L, n_ctx=n_ctx)
```

```python
import functools
import math

import jax
import jax.numpy as jnp
from jax import lax
from jax.experimental import pallas as pl
from jax.experimental.pallas import tpu as pltpu

F32 = jnp.float32
ACT = jnp.bfloat16
EPS = 1e-6
N_MOD = 6
GLA_HEADS = 4
GLA_RANK = 16
GLA_GATE_NORM = 16.0
GLA_CHUNK = 64
GLA_HEADS_PER_STEP = 2
CONV_WIDTH = 31
CONV_HALO = 16
DIFF_DK = 128
DIFF_DV = 256
GRID_W = 64
ROPE_BASE = 10000.0
LANE = 128
SUBLANE = 8
MOD_ROWS = 8
VMEM_LIMIT = 56 * 2 ** 20
ATTN_CHUNK = 256

def _params(n_axes):
    return pltpu.CompilerParams(dimension_semantics=("arbitrary",) * n_axes,
                                vmem_limit_bytes=VMEM_LIMIT)


def _dot(a, b):
    return jnp.dot(a, b, preferred_element_type=F32)


def _dot_nt(a, b):
    return lax.dot_general(a, b, (((1,), (1,)), ((), ())), preferred_element_type=F32)


def _dot_tn(a, b):
    return lax.dot_general(a, b, (((0,), (0,)), ((), ())), preferred_element_type=F32)


def _sigmoid(x):
    return 1.0 / (1.0 + jnp.exp(-x))


def _rms(x):
    return x * lax.rsqrt(jnp.mean(x * x, axis=-1, keepdims=True) + EPS)


def _row_tile(L, n_ctx, cap):
    best = n_ctx
    for m in range(1, L // n_ctx + 1):
        t = m * n_ctx
        if t <= cap and L % t == 0:
            best = t
    return best


def _col_tile(n, cap):
    best = LANE
    t = LANE
    while t <= min(n, cap):
        if n % t == 0:
            best = t
        t += LANE
    return best


def _mod_pick(ml_ref, mc_ref, k, D, ctx_row):
    lat = ml_ref[:, k * D:(k + 1) * D]
    ctx = mc_ref[:, k * D:(k + 1) * D]
    return jnp.where(ctx_row, ctx, lat)


def _ctx_rows(tm, n_ctx, first_tile):
    rows = lax.broadcasted_iota(jnp.int32, (tm, 1), 0)
    return jnp.logical_and(first_tile, rows < n_ctx)


def _norm_mod_store(h_ref, g_ref, ml_ref, mc_ref, k_shift, k_scale, first_tile, n_ctx, store):
    h = h_ref[...]
    tm, D = h.shape
    y = _rms(h)
    g = g_ref[...]
    lat_a = g * (1.0 + ml_ref[:, k_scale * D:(k_scale + 1) * D])
    lat_b = ml_ref[:, k_shift * D:(k_shift + 1) * D]

    @pl.when(jnp.logical_not(first_tile))
    def _():
        store(y * lat_a + lat_b)

    @pl.when(first_tile)
    def _():
        ctx_row = lax.broadcasted_iota(jnp.int32, (tm, 1), 0) < n_ctx
        ctx_a = g * (1.0 + mc_ref[:, k_scale * D:(k_scale + 1) * D])
        ctx_b = mc_ref[:, k_shift * D:(k_shift + 1) * D]
        store(y * jnp.where(ctx_row, ctx_a, lat_a) + jnp.where(ctx_row, ctx_b, lat_b))


def _mod_kernel(c_ref, w_ref, b_ref, o_ref):
    c = c_ref[...]
    sc = (c * _sigmoid(c)).astype(ACT)
    o_ref[...] = _dot(sc, w_ref[...].astype(ACT)) + b_ref[...]


def _mod_vectors(cvec, mod_w, mod_b):
    depth, D, N = mod_w.shape
    tn = _col_tile(N, 1024)
    return pl.pallas_call(
        _mod_kernel,
        grid=(depth, N // tn),
        in_specs=[pl.BlockSpec((MOD_ROWS, D), lambda l, j: (0, 0)),
                  pl.BlockSpec((None, D, tn), lambda l, j: (l, 0, j)),
                  pl.BlockSpec((None, 1, tn), lambda l, j: (l, 0, j))],
        out_specs=pl.BlockSpec((None, MOD_ROWS, tn), lambda l, j: (l, 0, j)),
        out_shape=jax.ShapeDtypeStruct((depth, MOD_ROWS, N), F32),
        compiler_params=_params(2),
        name="mod_vectors",
    )(cvec, mod_w, mod_b.reshape(depth, 1, N))


def _rope_tile(acc, cos, sin):
    return acc * cos + pltpu.roll(acc, DIFF_DK // 2, axis=1) * sin


def _proj_kernel(*refs, mode, tpb, n_ctx, n_rope_tiles, q_tiles, q_scale):
    if mode == "glu":
        h_ref, g_ref, ml_ref, mc_ref, wa_ref, wg_ref, ba_ref, bg_ref, o_ref, u_scr = refs
    elif mode == "rope":
        h_ref, g_ref, ml_ref, mc_ref, w_ref, cos_ref, sin_ref, o_ref, u_scr = refs
    else:
        h_ref, g_ref, ml_ref, mc_ref, w_ref, wa1_ref, o_ref, t_ref, u_scr = refs
    i = pl.program_id(0)
    j = pl.program_id(1)

    @pl.when(j == 0)
    def _():
        def store(u):
            u_scr[...] = u.astype(ACT)
        _norm_mod_store(h_ref, g_ref, ml_ref, mc_ref, 0, 1, i % tpb == 0, n_ctx, store)
        if mode == "gla":
            t_ref[...] = _dot(u_scr[...], wa1_ref[...]).astype(t_ref.dtype)

    u = u_scr[...]
    if mode == "glu":
        a = _dot(u, wa_ref[...]) + ba_ref[...]
        g = _dot(u, wg_ref[...]) + bg_ref[...]
        o_ref[...] = (a * _sigmoid(g)).astype(o_ref.dtype)
    elif mode == "rope":
        acc = _dot(u, w_ref[...])

        @pl.when(j < n_rope_tiles)
        def _():
            scale = jnp.where(j < q_tiles, q_scale, 1.0)
            cos, sin = cos_ref[...], sin_ref[...]
            for s in range(acc.shape[1] // LANE):
                seg = acc[:, s * LANE:(s + 1) * LANE]
                o_ref[:, s * LANE:(s + 1) * LANE] = (
                    _rope_tile(seg, cos, sin) * scale).astype(o_ref.dtype)

        @pl.when(j >= n_rope_tiles)
        def _():
            o_ref[...] = acc.astype(o_ref.dtype)
    else:
        o_ref[...] = _dot(u, w_ref[...]).astype(o_ref.dtype)


def _project(hs, g, ml, mc, w, *, L, n_ctx, mode, bias=None, rope=None, wa1=None,
             n_rope_cols=0, n_q_cols=0, q_scale=1.0):
    M, D = hs.shape
    N = w.shape[1]
    n_out = N // 2 if mode == "glu" else N
    tm = _row_tile(L, n_ctx, 768)
    tpb = L // tm
    tn = _col_tile(n_out, 1024)
    if mode == "rope":
        while n_rope_cols % tn or n_q_cols % tn:
            tn -= LANE
    nj = n_out // tn
    row = lambda i, j: (i, 0)
    fixed = lambda i, j: (0, 0)
    in_specs = [pl.BlockSpec((tm, D), row),
                pl.BlockSpec((1, D), fixed),
                pl.BlockSpec((None, 1, N_MOD * D), lambda i, j: (i // tpb, 0, 0)),
                pl.BlockSpec((1, N_MOD * D), fixed)]
    args = [hs, g.reshape(1, D), ml, mc]
    if mode == "glu":
        in_specs += [pl.BlockSpec((D, tn), lambda i, j: (0, j)),
                     pl.BlockSpec((D, tn), lambda i, j: (0, j + nj)),
                     pl.BlockSpec((1, tn), lambda i, j: (0, j)),
                     pl.BlockSpec((1, tn), lambda i, j: (0, j + nj))]
        b2 = bias.reshape(1, N)
        args += [w, w, b2, b2]
    else:
        in_specs.append(pl.BlockSpec((D, tn), lambda i, j: (0, j)))
        args.append(w)
        if mode == "rope":
            tab = pl.BlockSpec((tm, LANE), lambda i, j: (i % tpb, 0))
            in_specs += [tab, tab]
            args += list(rope)
    out_specs = pl.BlockSpec((tm, tn), lambda i, j: (i, j))
    out_shape = jax.ShapeDtypeStruct((M, n_out), ACT)
    if mode == "gla":
        in_specs.append(pl.BlockSpec((D, LANE), fixed))
        args.append(wa1)
        out_specs = [out_specs, pl.BlockSpec((tm, LANE), row)]
        out_shape = [out_shape, jax.ShapeDtypeStruct((M, LANE), ACT)]
    kern = functools.partial(_proj_kernel, mode=mode, tpb=tpb, n_ctx=n_ctx,
                             n_rope_tiles=n_rope_cols // tn, q_tiles=n_q_cols // tn,
                             q_scale=q_scale)
    return pl.pallas_call(
        kern,
        grid=(M // tm, nj),
        in_specs=in_specs,
        out_specs=out_specs,
        out_shape=out_shape,
        scratch_shapes=[pltpu.VMEM((tm, D), ACT)],
        compiler_params=_params(2),
        name="project_" + mode,
    )(*args)


def _mlp_kernel(h_ref, g_ref, ml_ref, mc_ref, w1_ref, w2_ref, o_ref, u_scr, *, tpb, n_ctx):
    i = pl.program_id(0)
    j = pl.program_id(1)
    tm, D = h_ref.shape
    first_tile = i % tpb == 0

    @pl.when(j == 0)
    def _():
        def store(u):
            u_scr[...] = u.astype(ACT)
        _norm_mod_store(h_ref, g_ref, ml_ref, mc_ref, 3, 4, first_tile, n_ctx, store)
        o_ref[...] = jnp.zeros_like(o_ref)

    hid = jnp.maximum(_dot(u_scr[...], w1_ref[...]), 0.0)
    o_ref[...] += _dot((hid * hid).astype(ACT), w2_ref[...])

    @pl.when(j == pl.num_programs(1) - 1)
    def _():
        gate = _mod_pick(ml_ref, mc_ref, 5, D, _ctx_rows(tm, n_ctx, first_tile))
        o_ref[...] = h_ref[...] + gate * o_ref[...]


def _mlp(hs, g, ml, mc, w1, w2, layer, *, L, n_ctx):
    M, D = hs.shape
    Hd = w1.shape[2]
    tm = _row_tile(L, n_ctx, 768)
    tpb = L // tm
    th = _col_tile(Hd, 1024)
    row = lambda i, j: (i, 0)
    fixed = lambda i, j: (0, 0)
    return pl.pallas_call(
        functools.partial(_mlp_kernel, tpb=tpb, n_ctx=n_ctx),
        grid=(M // tm, Hd // th),
        in_specs=[pl.BlockSpec((tm, D), row),
                  pl.BlockSpec((1, D), fixed),
                  pl.BlockSpec((None, 1, N_MOD * D), lambda i, j: (i // tpb, 0, 0)),
                  pl.BlockSpec((1, N_MOD * D), fixed),
                  pl.BlockSpec((None, D, th), lambda i, j: (layer, 0, j)),
                  pl.BlockSpec((None, th, D), lambda i, j: (layer, j, 0))],
        out_specs=pl.BlockSpec((tm, D), row),
        out_shape=jax.ShapeDtypeStruct((M, D), F32),
        scratch_shapes=[pltpu.VMEM((tm, D), ACT)],
        compiler_params=_params(2),
        name="mlp",
    )(hs, g.reshape(1, D), ml, mc, w1, w2)


def _out_kernel(*refs, mode, tpb, n_ctx, n_heads):
    if mode == "gla":
        h_ref, ml_ref, mc_ref, w_ref, of_ref, ob_ref, gt_ref, ng_ref, o_ref = refs
    else:
        h_ref, ml_ref, mc_ref, w_ref, aa_ref, ab_ref, o_ref = refs
    i = pl.program_id(0)
    tm, D = h_ref.shape
    ctx_row = _ctx_rows(tm, n_ctx, i % tpb == 0)
    if mode == "gla":
        o = of_ref[...].astype(F32) + ob_ref[...].astype(F32)
        dv = o.shape[1] // n_heads
        ng = ng_ref[...]
        o = jnp.concatenate([_rms(o[:, hh * dv:(hh + 1) * dv]) * ng for hh in range(n_heads)],
                            axis=1)
        gt = gt_ref[...].astype(F32)
        a = (o * (gt * _sigmoid(gt))).astype(ACT)
    else:
        t = i % tpb
        a = jnp.where(jnp.logical_and(t >= 2, t % 2 == 0), ab_ref[...], aa_ref[...])
    gate = _mod_pick(ml_ref, mc_ref, 2, D, ctx_row)
    o_ref[...] = h_ref[...] + gate * _dot(a, w_ref[...])


def _out_residual(hs, ml, mc, w, *, L, n_ctx, a=None, gla=None):
    M, D = hs.shape
    K = w.shape[0]
    tm = _row_tile(L, n_ctx, 256)
    tpb = L // tm
    row = lambda i: (i, 0)
    fixed = lambda i: (0, 0)
    in_specs = [pl.BlockSpec((tm, D), row),
                pl.BlockSpec((None, 1, N_MOD * D), lambda i: (i // tpb, 0, 0)),
                pl.BlockSpec((1, N_MOD * D), fixed),
                pl.BlockSpec((K, D), fixed)]
    args = [hs, ml, mc, w]
    if gla is not None:
        o_f, o_b, qkvg, norm_g = gla
        g_blk = qkvg.shape[1] // K - 1
        in_specs += [pl.BlockSpec((tm, K), row), pl.BlockSpec((tm, K), row),
                     pl.BlockSpec((tm, K), lambda i: (i, g_blk)),
                     pl.BlockSpec((1, norm_g.shape[0]), fixed)]
        args += [o_f, o_b, qkvg, norm_g.reshape(1, -1)]
        mode = "gla"
    else:
        assert tm == n_ctx and (tpb - 1) % 2 == 0
        na, nb = (tpb + 1) // 2, (tpb - 1) // 2
        in_specs += [pl.BlockSpec((tm, K), lambda i: ((i // tpb) * na + (i % tpb + 1) // 2, 0)),
                     pl.BlockSpec((tm, K), lambda i: (
                         (i // tpb) * nb + jnp.maximum((i % tpb) // 2 - 1, 0), 0))]
        args += list(a)
        mode = "pair"
    return pl.pallas_call(
        functools.partial(_out_kernel, mode=mode, tpb=tpb, n_ctx=n_ctx, n_heads=GLA_HEADS),
        grid=(M // tm,),
        in_specs=in_specs,
        out_specs=pl.BlockSpec((tm, D), row),
        out_shape=jax.ShapeDtypeStruct((M, D), F32),
        compiler_params=_params(1),
        name="out_" + mode,
    )(*args)


def _gla_direction(t_ref, wa2_ref, ba_ref, q_ref, k_ref, v_ref, o_ref, s_scr,
                   la_scr, qd_scr, qa_scr, qf_scr, kr_scr, km_scr, oi_scr, af_scr,
                   *, forward, q_scale):
    C = GLA_CHUNK
    blk = q_ref.shape[0]
    n = blk // C
    dk, dv = s_scr.shape
    order = list(range(n)) if forward else list(range(n - 1, -1, -1))
    last_row = C - 1 if forward else 0
    r = lax.broadcasted_iota(jnp.int32, (blk, blk), 0)
    c = lax.broadcasted_iota(jnp.int32, (blk, blk), 1)
    causal = (c <= r) if forward else (c >= r)
    shift = C.bit_length() - 1
    same_chunk = jnp.right_shift(r, shift) == jnp.right_shift(c, shift)
    tri = jnp.logical_and(causal, same_chunk).astype(ACT)

    z = _dot(t_ref[...], wa2_ref[...]) + ba_ref[...]
    g = (jnp.minimum(z, 0.0) - jnp.log(1.0 + jnp.exp(-jnp.abs(z)))) / GLA_GATE_NORM
    g_hi = g.astype(ACT)
    g_lo = (g - g_hi.astype(F32)).astype(ACT)
    la_scr[...] = _dot(tri, g_hi) + _dot(tri, g_lo)

    totals = []
    for ci in range(n):
        rows = pl.ds(ci * C, C)
        b = la_scr[rows, :]
        total = b[last_row:last_row + 1, :]
        totals.append(total)
        q = q_ref[rows, :].astype(F32) * q_scale
        k = k_ref[rows, :].astype(F32)
        q_dec = q * jnp.exp(b)
        qd_scr[rows, :] = q_dec
        qa_scr[rows, :] = q_dec.astype(ACT)
        km_scr[rows, :] = (k * jnp.exp(-b)).astype(ACT)
        kr_scr[rows, :] = k * jnp.exp(total - b)

    run = None
    for ci in order:
        rows = pl.ds(ci * C, C)
        q_dec = qd_scr[rows, :]
        qf_scr[rows, :] = (q_dec if run is None else q_dec * jnp.exp(run)).astype(ACT)
        run = totals[ci] if run is None else run + totals[ci]
    state = s_scr[...]
    oi_scr[...] = _dot(qf_scr[...], state.astype(ACT))

    def keys_at(pos, cj):
        gap = None
        for cm in order[order.index(cj) + 1:pos]:
            gap = totals[cm] if gap is None else gap + totals[cm]
        k_end = kr_scr[pl.ds(cj * C, C), :]
        return k_end if gap is None else k_end * jnp.exp(gap)

    for pos, ci in enumerate(order):
        rows = pl.ds(ci * C, C)
        keys = jnp.concatenate(
            [keys_at(pos, cj).astype(ACT) if cj in order[:pos] else km_scr[pl.ds(cj * C, C), :]
             for cj in range(n)], axis=0)
        att = _dot_nt(qa_scr[rows, :], keys)
        af_scr[rows, :] = jnp.where(causal[ci * C:(ci + 1) * C, :], att, 0.0).astype(ACT)

    k_fin = jnp.concatenate([keys_at(n, cj) for cj in range(n)], axis=0)
    both = _dot(jnp.concatenate([af_scr[...], k_fin.T.astype(ACT)], axis=0), v_ref[...])
    o_ref[...] = (both[:blk, :] + oi_scr[...]).astype(o_ref.dtype)
    decay_col = jnp.broadcast_to(jnp.exp(run), (LANE, dk)).T
    s_scr[...] = state * jnp.tile(decay_col, (1, dv // LANE)) + both[blk:, :]


def _gla_kernel(tf_ref, w2f_ref, baf_ref, qf_ref, kf_ref, vf_ref,
                tb_ref, w2b_ref, bab_ref, qb_ref, kb_ref, vb_ref,
                of_ref, ob_ref, sf_scr, sb_scr, *scratch, q_scale):
    @pl.when(pl.program_id(2) == 0)
    def _():
        sf_scr[...] = jnp.zeros_like(sf_scr)
        sb_scr[...] = jnp.zeros_like(sb_scr)

    heads, dk, dv = sf_scr.shape
    per = len(scratch) // (2 * heads)
    for hh in range(heads):
        kc, vc = pl.ds(hh * dk, dk), pl.ds(hh * dv, dv)
        _gla_direction(tf_ref, w2f_ref.at[:, kc], baf_ref.at[:, kc], qf_ref.at[:, kc],
                       kf_ref.at[:, kc], vf_ref.at[:, vc], of_ref.at[:, vc], sf_scr.at[hh],
                       *scratch[2 * hh * per:(2 * hh + 1) * per], forward=True, q_scale=q_scale)
        _gla_direction(tb_ref, w2b_ref.at[:, kc], bab_ref.at[:, kc], qb_ref.at[:, kc],
                       kb_ref.at[:, kc], vb_ref.at[:, vc], ob_ref.at[:, vc], sb_scr.at[hh],
                       *scratch[(2 * hh + 1) * per:(2 * hh + 2) * per], forward=False,
                       q_scale=q_scale)


def _gla_scan(qkvg, t, wa2, ba, *, B, L, n_ctx, dk, dv):
    M = qkvg.shape[0]
    H = GLA_HEADS
    hp = GLA_HEADS_PER_STEP
    blk = n_ctx
    nb = L // blk
    wk, wv = hp * dk, hp * dv
    kq = (H * dk) // wk
    kv = (2 * H * dk) // wv

    def fwd(n):
        return n

    def bwd(n):
        return jnp.where(n == 0, 0, nb - n)

    def direction_specs(d, order):
        rows = lambda b, n: b * nb + order(n)
        return [pl.BlockSpec((blk, LANE), lambda b, h, n: (rows(b, n), 0)),
                pl.BlockSpec((None, LANE, wk), lambda b, h, n: (d, 0, h)),
                pl.BlockSpec((None, 1, wk), lambda b, h, n: (d, 0, h)),
                pl.BlockSpec((blk, wk), lambda b, h, n: (rows(b, n), h)),
                pl.BlockSpec((blk, wk), lambda b, h, n: (rows(b, n), kq + h)),
                pl.BlockSpec((blk, wv), lambda b, h, n: (rows(b, n), kv + h))]

    out_spec = lambda order: pl.BlockSpec((blk, wv), lambda b, h, n: (b * nb + order(n), h))
    stage = [pltpu.VMEM((blk, dk), F32), pltpu.VMEM((blk, dk), F32), pltpu.VMEM((blk, dk), ACT),
             pltpu.VMEM((blk, dk), ACT), pltpu.VMEM((blk, dk), F32), pltpu.VMEM((blk, dk), ACT),
             pltpu.VMEM((blk, dv), F32), pltpu.VMEM((blk, blk), ACT)]
    state = pltpu.VMEM((hp, dk, dv), F32)
    return pl.pallas_call(
        functools.partial(_gla_kernel, q_scale=dk ** -0.5),
        grid=(B, H // hp, nb),
        in_specs=direction_specs(0, fwd) + direction_specs(1, bwd),
        out_specs=[out_spec(fwd), out_spec(bwd)],
        out_shape=[jax.ShapeDtypeStruct((M, H * dv), ACT)] * 2,
        scratch_shapes=[state, state] + stage * (2 * hp),
        compiler_params=_params(3),
        name="gla_scan",
    )(t, wa2, ba, qkvg, qkvg, qkvg, t, wa2, ba, qkvg, qkvg, qkvg)


def _conv_kernel(h_ref, ml_ref, mc_ref, x_ref, prev_ref, next_ref, dw_ref, dwb_ref,
                 lng_ref, lnb_ref, w_ref, b_ref, o_ref, xw_scr, y_scr, *, tpb):
    i = pl.program_id(0)
    tm, D = h_ref.shape
    t = i % tpb
    has_prev = t > 1
    has_next = jnp.logical_and(t > 0, t < tpb - 1)
    xw_scr[0, 0:CONV_HALO, :] = jnp.where(has_prev, prev_ref[...].astype(F32), 0.0)
    xw_scr[0, CONV_HALO:CONV_HALO + tm, :] = x_ref[...].astype(F32)
    xw_scr[0, CONV_HALO + tm:, :] = jnp.where(has_next, next_ref[...].astype(F32), 0.0)
    n_shift = tm + 2 * CONV_HALO - SUBLANE
    for r in range(1, SUBLANE):
        xw_scr[r, 0:n_shift, :] = xw_scr[0, r:r + n_shift, :]
    first = CONV_HALO - CONV_WIDTH // 2
    rb = min(tm, 128)

    def col_body(cb, carry):
        cols = pl.ds(pl.multiple_of(cb * LANE, LANE), LANE)
        for r0 in range(0, tm, rb):
            acc = jnp.zeros((rb, LANE), F32)
            for k in range(CONV_WIDTH):
                a, r = divmod(first + k, SUBLANE)
                tap = xw_scr[r, pl.ds(r0 + a * SUBLANE, rb), cols]
                acc = acc + tap * dw_ref[pl.ds(k, 1), cols]
            y_scr[pl.ds(r0, rb), cols] = acc + dwb_ref[:, cols]
        return carry

    lax.fori_loop(0, D // LANE, col_body, 0)
    y = y_scr[...]
    yc = y - jnp.mean(y, axis=-1, keepdims=True)
    yn = yc * lax.rsqrt(jnp.mean(yc * yc, axis=-1, keepdims=True) + EPS)
    yn = yn * lng_ref[...] + lnb_ref[...]
    a = (yn * _sigmoid(yn)).astype(ACT)
    gate = _mod_pick(ml_ref, mc_ref, 2, D, t == 0)
    o_ref[...] = h_ref[...] + gate * (_dot(a, w_ref[...]) + b_ref[...])


def _conv_out(hs, ml, mc, glu, dw, dwb, ln_g, ln_b, w2, b2, *, L, n_ctx):
    M, D = hs.shape
    tm = n_ctx
    tpb = L // tm
    hb = tm // CONV_HALO
    n_halo = M // CONV_HALO
    row = lambda i: (i, 0)
    fixed = lambda i: (0, 0)
    vec = pl.BlockSpec((1, D), fixed)
    return pl.pallas_call(
        functools.partial(_conv_kernel, tpb=tpb),
        grid=(M // tm,),
        in_specs=[pl.BlockSpec((tm, D), row),
                  pl.BlockSpec((None, 1, N_MOD * D), lambda i: (i // tpb, 0, 0)),
                  pl.BlockSpec((1, N_MOD * D), fixed),
                  pl.BlockSpec((tm, D), row),
                  pl.BlockSpec((CONV_HALO, D), lambda i: (jnp.maximum(i * hb - 1, 0), 0)),
                  pl.BlockSpec((CONV_HALO, D), lambda i: (jnp.minimum((i + 1) * hb, n_halo - 1), 0)),
                  pl.BlockSpec((CONV_WIDTH, D), fixed),
                  vec, vec, vec,
                  pl.BlockSpec((D, D), fixed),
                  vec],
        out_specs=pl.BlockSpec((tm, D), row),
        out_shape=jax.ShapeDtypeStruct((M, D), F32),
        scratch_shapes=[pltpu.VMEM((SUBLANE, tm + 2 * CONV_HALO, D), F32),
                        pltpu.VMEM((tm, D), F32)],
        compiler_params=_params(1),
        name="conv_out",
    )(hs, ml, mc, glu, glu, glu, dw, dwb.reshape(1, D), ln_g.reshape(1, D), ln_b.reshape(1, D),
      w2, b2.reshape(1, D))


def _attn_tile(q_ref, o_ref, k_ref, lq1_ref, lk1_ref, lq2_ref, lk2_ref, sg_ref,
               vtc_scr, vtl_scr, acc_scr, m_scr, l_scr, s0_scr, s1_scr, p0_scr, p1_scr,
               a0_scr, a1_scr, x0_scr, x1_scr, *, n_ctx, tk, lambda_init):
    n_lat_steps = vtl_scr.shape[0]

    def context_keys():
        vt = vtc_scr[...]
        for j in range(2):
            lanes = slice(j * DIFF_DK, (j + 1) * DIFF_DK)
            st = _dot_nt(k_ref[0:n_ctx, lanes], q_ref[:, lanes])
            m = jnp.max(st, axis=0, keepdims=True)
            p = jnp.exp2(st - m)
            l_scr[j] = jnp.sum(p, axis=0, keepdims=True)
            acc_scr[j] = _dot(vt, p.astype(ACT))
            m_scr[j] = m

    def finish():
        lam = (jnp.exp(jnp.sum(lq1_ref[...] * lk1_ref[...], axis=-1, keepdims=True))
               - jnp.exp(jnp.sum(lq2_ref[...] * lk2_ref[...], axis=-1, keepdims=True))
               + lambda_init)
        ot = acc_scr[0] * (1.0 / l_scr[0]) - lam * (acc_scr[1] * (1.0 / l_scr[1]))
        ot = ot * lax.rsqrt(jnp.mean(ot * ot, axis=0, keepdims=True) + EPS)
        ot = ot * (sg_ref[...] * (1.0 - lambda_init))
        o_ref[...] = ot.T.astype(o_ref.dtype)

    def scores(t, s_scr, x_scr):
        for j in range(2):
            lanes = slice(j * DIFF_DK, (j + 1) * DIFF_DK)
            part = None
            for c0 in range(0, tk, ATTN_CHUNK):
                rows = pl.ds(pl.multiple_of(n_ctx + t * tk + c0, ATTN_CHUNK), ATTN_CHUNK)
                st = _dot_nt(k_ref[rows, lanes], q_ref[:, lanes])
                s_scr[j, c0:c0 + ATTN_CHUNK, :] = st
                part = st if part is None else jnp.maximum(part, st)
            x_scr[j] = jnp.max(part, axis=0, keepdims=True)

    def softmax(s_scr, x_scr, p_scr, a_scr):
        for j in range(2):
            m_prev = m_scr[j]
            m_new = jnp.maximum(m_prev, x_scr[j])
            alpha = jnp.exp2(m_prev - m_new)
            part = None
            for c0 in range(0, tk, ATTN_CHUNK):
                p = jnp.exp2(s_scr[j, c0:c0 + ATTN_CHUNK, :] - m_new)
                p_scr[j, c0:c0 + ATTN_CHUNK, :] = p.astype(ACT)
                part = p if part is None else part + p
            l_scr[j] = alpha * l_scr[j] + jnp.sum(part, axis=0, keepdims=True)
            m_scr[j] = m_new
            a_scr[j] = alpha

    def values(t, p_scr, a_scr):
        vt = vtl_scr[t]
        for j in range(2):
            acc_scr[j] = a_scr[j] * acc_scr[j] + _dot(vt, p_scr[j])

    def latent_keys():
        scores(0, s0_scr, x0_scr)
        scores(1, s1_scr, x1_scr)
        softmax(s0_scr, x0_scr, p0_scr, a0_scr)

        def body(i, carry):
            t = 2 * i
            scores(t + 2, s0_scr, x0_scr)
            softmax(s1_scr, x1_scr, p1_scr, a1_scr)
            values(t, p0_scr, a0_scr)
            scores(t + 3, s1_scr, x1_scr)
            softmax(s0_scr, x0_scr, p0_scr, a0_scr)
            values(t + 1, p1_scr, a1_scr)
            return carry
        lax.fori_loop(0, n_lat_steps // 2 - 1, body, 0, unroll=True)
        softmax(s1_scr, x1_scr, p1_scr, a1_scr)
        values(n_lat_steps - 2, p0_scr, a0_scr)
        values(n_lat_steps - 1, p1_scr, a1_scr)

    return context_keys, latent_keys, finish


def _attn_kernel(qa_ref, qb_ref, k_ref, v_ref, lq1_ref, lk1_ref, lq2_ref, lk2_ref, sg_ref,
                 oa_ref, ob_ref, vtc_scr, vtl_scr, *scratch, n_ctx, tk, lambda_init):
    per = len(scratch) // 2
    shared = (k_ref, lq1_ref, lk1_ref, lq2_ref, lk2_ref, sg_ref, vtc_scr, vtl_scr)
    consts = dict(n_ctx=n_ctx, tk=tk, lambda_init=lambda_init)
    ctx_a, lat_a, fin_a = _attn_tile(qa_ref, oa_ref, *shared, *scratch[:per], **consts)
    ctx_b, lat_b, fin_b = _attn_tile(qb_ref, ob_ref, *shared, *scratch[per:], **consts)
    first = pl.program_id(2) == 0

    @pl.when(first)
    def _():
        vtc_scr[...] = v_ref[0:n_ctx, :].astype(F32).T.astype(ACT)

        def fill(t, carry):
            rows = pl.ds(pl.multiple_of(n_ctx + t * tk, n_ctx), tk)
            vtl_scr[t] = v_ref[rows, :].astype(F32).T.astype(ACT)
            return carry
        lax.fori_loop(0, vtl_scr.shape[0], fill, 0)
        ctx_a()
        fin_a()
        ob_ref[...] = jnp.zeros_like(ob_ref)

    @pl.when(jnp.logical_not(first))
    def _():
        ctx_a()
        lat_a()
        fin_a()
        ctx_b()
        lat_b()
        fin_b()


def _diff_attention(qkv, lq1, lk1, lq2, lk2, subln_g, *, B, L, n_ctx, lambda_init):
    M, N = qkv.shape
    width = 2 * DIFF_DK
    H = N // (3 * width)
    tq = n_ctx
    nq = L // tq
    assert (nq - 1) % 2 == 0
    n_pairs = (nq - 1) // 2
    tk = _row_tile(L - n_ctx, n_ctx, 512)
    n_steps = (L - n_ctx) // tk
    assert n_steps >= 2 and n_steps % 2 == 0
    stage = lambda dtype: pltpu.VMEM((2, tk, tq), dtype)
    stat = pltpu.VMEM((2, 1, tq), F32)
    per_tile = [pltpu.VMEM((2, DIFF_DV, tq), F32), stat, stat,
                stage(F32), stage(F32), stage(ACT), stage(ACT), stat, stat, stat, stat]
    fixed = lambda b, h, s: (0, 0)
    vec = pl.BlockSpec((1, DIFF_DK), fixed)
    tile_a = lambda s: jnp.maximum(2 * s - 1, 0)
    tile_b = lambda s: jnp.maximum(2 * s, 1)
    return pl.pallas_call(
        functools.partial(_attn_kernel, n_ctx=n_ctx, tk=tk, lambda_init=lambda_init),
        grid=(B, H, n_pairs + 1),
        in_specs=[pl.BlockSpec((tq, width), lambda b, h, s: (b * nq + tile_a(s), h)),
                  pl.BlockSpec((tq, width), lambda b, h, s: (b * nq + tile_b(s), h)),
                  pl.BlockSpec((L, width), lambda b, h, s: (b, H + h)),
                  pl.BlockSpec((L, DIFF_DV), lambda b, h, s: (b, 2 * H + h)),
                  vec, vec, vec, vec,
                  pl.BlockSpec((DIFF_DV, 1), fixed)],
        out_specs=[pl.BlockSpec((tq, DIFF_DV), lambda b, h, s: (b * (n_pairs + 1) + s, h)),
                   pl.BlockSpec((tq, DIFF_DV),
                                lambda b, h, s: (b * n_pairs + jnp.maximum(s - 1, 0), h))],
        out_shape=[jax.ShapeDtypeStruct((B * (n_pairs + 1) * tq, H * DIFF_DV), ACT),
                   jax.ShapeDtypeStruct((B * n_pairs * tq, H * DIFF_DV), ACT)],
        scratch_shapes=[pltpu.VMEM((DIFF_DV, n_ctx), ACT),
                        pltpu.VMEM((n_steps, DIFF_DV, tk), ACT)] + per_tile + per_tile,
        compiler_params=_params(3),
        name="diff_attention",
    )(qkv, qkv, qkv, qkv, lq1.reshape(1, -1), lk1.reshape(1, -1), lq2.reshape(1, -1),
      lk2.reshape(1, -1), subln_g.reshape(-1, 1))


def _rope_layout():
    quarter = DIFF_DK // 4
    idx = jnp.arange(DIFF_DK).reshape(4, quarter)
    return idx[jnp.array([0, 2, 1, 3])].reshape(-1)


def _rope_tables(n_ctx, n_lat):
    axis_dim = DIFF_DK // 2
    rows = n_lat // GRID_W
    pos_row = jnp.repeat(jnp.arange(rows, dtype=F32), GRID_W)
    pos_col = jnp.tile(jnp.arange(GRID_W, dtype=F32), rows)
    inv_freq = ROPE_BASE ** (-jnp.arange(0, axis_dim, 2, dtype=F32) / axis_dim)
    ang_r = pos_row[:, None] * inv_freq
    ang_c = pos_col[:, None] * inv_freq
    ang = jnp.concatenate([ang_r, ang_r, ang_c, ang_c], axis=-1)
    cos = jnp.concatenate([jnp.ones((n_ctx, DIFF_DK), F32), jnp.cos(ang)], axis=0)
    sin = jnp.concatenate([jnp.zeros((n_ctx, DIFF_DK), F32), jnp.sin(ang)], axis=0)
    first_half = (jnp.arange(DIFF_DK) % (axis_dim)) < axis_dim // 2
    order = _rope_layout()
    return cos[:, order], jnp.where(first_half, -sin, sin)[:, order]


def _final_kernel(h_ref, g_ref, o_ref):
    o_ref[...] = _rms(h_ref[...]) * g_ref[...]


def _final_norm(hs, g, *, B, L, n_ctx):
    D = hs.shape[1]
    tm = n_ctx
    skip = n_ctx // tm
    n_lat = L - n_ctx
    return pl.pallas_call(
        _final_kernel,
        grid=(B, n_lat // tm),
        in_specs=[pl.BlockSpec((None, tm, D), lambda b, i: (b, i + skip, 0)),
                  pl.BlockSpec((1, D), lambda b, i: (0, 0))],
        out_specs=pl.BlockSpec((None, tm, D), lambda b, i: (b, i, 0)),
        out_shape=jax.ShapeDtypeStruct((B, n_lat, D), F32),
        compiler_params=_params(2),
        name="final_norm",
    )(hs.reshape(B, L, D), g.reshape(1, D))


def kernel(x, c, ctx, c_ctx, mod_w, mod_b, norm_mix_g, norm_mlp_g, mlp_w1, mlp_w2, gla_w_in, gla_wa1_f, gla_wa2_f, gla_ba_f, gla_wa1_b, gla_wa2_b, gla_ba_b, gla_norm_g, gla_w_o, conv_w1, conv_b1, conv_dw, conv_dwb, conv_ln_g, conv_ln_b, conv_w2, conv_b2, diff_w_qkv, diff_lq1, diff_lk1, diff_lq2, diff_lk2, diff_subln_g, diff_w_o, final_g):
    B, n_lat, D = x.shape
    n_ctx = ctx.shape[1]
    L = n_ctx + n_lat
    depth = mod_w.shape[0]
    assert B + 1 <= MOD_ROWS and n_lat % n_ctx == 0 and n_ctx % GLA_CHUNK == 0
    dims = dict(L=L, n_ctx=n_ctx)

    hs = jnp.concatenate([ctx, x], axis=1).reshape(B * L, D)
    cvec = jnp.zeros((MOD_ROWS, D), F32).at[:B].set(c).at[B].set(c_ctx)
    mods = _mod_vectors(cvec, mod_w, mod_b)
    rope = _rope_tables(n_ctx, n_lat)
    w1_all, w2_all = mlp_w1.astype(ACT), mlp_w2.astype(ACT)

    for i in range(depth):
        ml = mods[i, :B][:, None, :]
        mc = mods[i, B:B + 1]
        kind, j = i % 3, i // 3
        if kind == 0:
            hk = gla_wa2_f.shape[2]
            dk, dv = hk // GLA_HEADS, gla_w_o.shape[1] // GLA_HEADS
            wa1 = jnp.zeros((D, LANE), F32).at[:, :GLA_RANK].set(gla_wa1_f[j])
            wa1 = wa1.at[:, GLA_RANK:2 * GLA_RANK].set(gla_wa1_b[j])
            wa2 = jnp.zeros((2, LANE, hk), F32).at[0, :GLA_RANK].set(gla_wa2_f[j])
            wa2 = wa2.at[1, GLA_RANK:2 * GLA_RANK].set(gla_wa2_b[j])
            ba = jnp.stack([gla_ba_f[j], gla_ba_b[j]])[:, None, :]
            qkvg, t = _project(hs, norm_mix_g[i], ml, mc, gla_w_in[j].astype(ACT), mode="gla",
                               wa1=wa1.astype(ACT), **dims)
            o_f, o_b = _gla_scan(qkvg, t, wa2.astype(ACT), ba, B=B, dk=dk, dv=dv, **dims)
            hs = _out_residual(hs, ml, mc, gla_w_o[j].astype(ACT),
                               gla=(o_f, o_b, qkvg, gla_norm_g[j]), **dims)
        elif kind == 1:
            glu = _project(hs, norm_mix_g[i], ml, mc, conv_w1[j].astype(ACT), mode="glu",
                           bias=conv_b1[j], **dims)
            hs = _conv_out(hs, ml, mc, glu, conv_dw[j], conv_dwb[j], conv_ln_g[j], conv_ln_b[j],
                           conv_w2[j].astype(ACT), conv_b2[j], **dims)
        else:
            lambda_init = 0.8 - 0.6 * math.exp(-0.3 * i)
            n_qk = diff_w_qkv.shape[2] // 3
            w_qk = diff_w_qkv[j][:, :2 * n_qk].reshape(D, -1, DIFF_DK)[:, :, _rope_layout()]
            w_qkv = jnp.concatenate([w_qk.reshape(D, 2 * n_qk), diff_w_qkv[j][:, 2 * n_qk:]], axis=1)
            qkv = _project(hs, norm_mix_g[i], ml, mc, w_qkv.astype(ACT), mode="rope",
                           rope=rope, n_rope_cols=2 * n_qk, n_q_cols=n_qk,
                           q_scale=DIFF_DK ** -0.5 * math.log2(math.e), **dims)
            att = _diff_attention(qkv, diff_lq1[j], diff_lk1[j], diff_lq2[j], diff_lk2[j],
                                  diff_subln_g[j], B=B, lambda_init=lambda_init, **dims)
            hs = _out_residual(hs, ml, mc, diff_w_o[j].astype(ACT), a=att, **dims)
        hs = _mlp(hs, norm_mlp_g[i], ml, mc, w1_all, w2_all, i, **dims)
    return _final_norm(hs, final_g, B=B, L=L, n_ctx=n_ctx)
```

```python
import functools
import math

import jax
import jax.numpy as jnp
from jax import lax
from jax.experimental import pallas as pl
from jax.experimental.pallas import tpu as pltpu

F32 = jnp.float32
ACT = jnp.bfloat16
EPS = 1e-6
N_MOD = 6
GLA_HEADS = 4
GLA_RANK = 16
GLA_GATE_NORM = 16.0
GLA_CHUNK = 64
GLA_HEADS_PER_STEP = 4
CONV_WIDTH = 31
CONV_HALO = 16
DIFF_DK = 128
DIFF_DV = 256
GRID_W = 64
ROPE_BASE = 10000.0
LANE = 128
SUBLANE = 8
MOD_ROWS = 8
VMEM_LIMIT = 56 * 2 ** 20
ATTN_CHUNK = 256

def _params(n_axes):
    return pltpu.CompilerParams(dimension_semantics=("arbitrary",) * n_axes,
                                vmem_limit_bytes=VMEM_LIMIT)


def _dot(a, b):
    return jnp.dot(a, b, preferred_element_type=F32)


def _dot_nt(a, b):
    return lax.dot_general(a, b, (((1,), (1,)), ((), ())), preferred_element_type=F32)


def _dot_tn(a, b):
    return lax.dot_general(a, b, (((0,), (0,)), ((), ())), preferred_element_type=F32)


def _sigmoid(x):
    return 1.0 / (1.0 + jnp.exp(-x))


def _rms(x):
    return x * lax.rsqrt(jnp.mean(x * x, axis=-1, keepdims=True) + EPS)


def _row_tile(L, n_ctx, cap):
    best = n_ctx
    for m in range(1, L // n_ctx + 1):
        t = m * n_ctx
        if t <= cap and L % t == 0:
            best = t
    return best


def _col_tile(n, cap):
    best = LANE
    t = LANE
    while t <= min(n, cap):
        if n % t == 0:
            best = t
        t += LANE
    return best


def _mod_pick(ml_ref, mc_ref, k, D, ctx_row):
    lat = ml_ref[:, k * D:(k + 1) * D]
    ctx = mc_ref[:, k * D:(k + 1) * D]
    return jnp.where(ctx_row, ctx, lat)


def _ctx_rows(tm, n_ctx, first_tile):
    rows = lax.broadcasted_iota(jnp.int32, (tm, 1), 0)
    return jnp.logical_and(first_tile, rows < n_ctx)


def _norm_mod_store(h_ref, g_ref, ml_ref, mc_ref, k_shift, k_scale, first_tile, n_ctx, store):
    h = h_ref[...]
    tm, D = h.shape
    y = _rms(h)
    g = g_ref[...]
    lat_a = g * (1.0 + ml_ref[:, k_scale * D:(k_scale + 1) * D])
    lat_b = ml_ref[:, k_shift * D:(k_shift + 1) * D]

    @pl.when(jnp.logical_not(first_tile))
    def _():
        store(y * lat_a + lat_b)

    @pl.when(first_tile)
    def _():
        ctx_row = lax.broadcasted_iota(jnp.int32, (tm, 1), 0) < n_ctx
        ctx_a = g * (1.0 + mc_ref[:, k_scale * D:(k_scale + 1) * D])
        ctx_b = mc_ref[:, k_shift * D:(k_shift + 1) * D]
        store(y * jnp.where(ctx_row, ctx_a, lat_a) + jnp.where(ctx_row, ctx_b, lat_b))


def _mod_kernel(c_ref, w_ref, b_ref, o_ref):
    c = c_ref[...]
    sc = (c * _sigmoid(c)).astype(ACT)
    o_ref[...] = _dot(sc, w_ref[...].astype(ACT)) + b_ref[...]


def _mod_vectors(cvec, mod_w, mod_b):
    depth, D, N = mod_w.shape
    tn = _col_tile(N, 1024)
    return pl.pallas_call(
        _mod_kernel,
        grid=(depth, N // tn),
        in_specs=[pl.BlockSpec((MOD_ROWS, D), lambda l, j: (0, 0)),
                  pl.BlockSpec((None, D, tn), lambda l, j: (l, 0, j)),
                  pl.BlockSpec((None, 1, tn), lambda l, j: (l, 0, j))],
        out_specs=pl.BlockSpec((None, MOD_ROWS, tn), lambda l, j: (l, 0, j)),
        out_shape=jax.ShapeDtypeStruct((depth, MOD_ROWS, N), F32),
        compiler_params=_params(2),
        name="mod_vectors",
    )(cvec, mod_w, mod_b.reshape(depth, 1, N))


def _rope_tile(acc, cos, sin):
    return acc * cos + pltpu.roll(acc, DIFF_DK // 2, axis=1) * sin


def _proj_kernel(*refs, mode, tpb, n_ctx, n_rope_tiles, q_tiles, q_scale):
    if mode == "glu":
        h_ref, g_ref, ml_ref, mc_ref, wa_ref, wg_ref, ba_ref, bg_ref, o_ref, u_scr = refs
    elif mode == "rope":
        h_ref, g_ref, ml_ref, mc_ref, w_ref, cos_ref, sin_ref, o_ref, u_scr = refs
    else:
        h_ref, g_ref, ml_ref, mc_ref, w_ref, wa1_ref, o_ref, t_ref, u_scr = refs
    i = pl.program_id(0)
    j = pl.program_id(1)

    @pl.when(j == 0)
    def _():
        def store(u):
            u_scr[...] = u.astype(ACT)
        _norm_mod_store(h_ref, g_ref, ml_ref, mc_ref, 0, 1, i % tpb == 0, n_ctx, store)
        if mode == "gla":
            t_ref[...] = _dot(u_scr[...], wa1_ref[...]).astype(t_ref.dtype)

    u = u_scr[...]
    if mode == "glu":
        a = _dot(u, wa_ref[...]) + ba_ref[...]
        g = _dot(u, wg_ref[...]) + bg_ref[...]
        o_ref[...] = (a * _sigmoid(g)).astype(o_ref.dtype)
    elif mode == "rope":
        acc = _dot(u, w_ref[...])

        @pl.when(j < n_rope_tiles)
        def _():
            scale = jnp.where(j < q_tiles, q_scale, 1.0)
            cos, sin = cos_ref[...], sin_ref[...]
            for s in range(acc.shape[1] // LANE):
                seg = acc[:, s * LANE:(s + 1) * LANE]
                o_ref[:, s * LANE:(s + 1) * LANE] = (
                    _rope_tile(seg, cos, sin) * scale).astype(o_ref.dtype)

        @pl.when(j >= n_rope_tiles)
        def _():
            o_ref[...] = acc.astype(o_ref.dtype)
    else:
        o_ref[...] = _dot(u, w_ref[...]).astype(o_ref.dtype)


def _project(hs, g, ml, mc, w, *, L, n_ctx, mode, bias=None, rope=None, wa1=None,
             n_rope_cols=0, n_q_cols=0, q_scale=1.0):
    M, D = hs.shape
    N = w.shape[1]
    n_out = N // 2 if mode == "glu" else N
    tm = _row_tile(L, n_ctx, 768)
    tpb = L // tm
    tn = _col_tile(n_out, 2048 if mode == "rope" else 1024)
    if mode == "rope":
        while n_rope_cols % tn or n_q_cols % tn:
            tn -= LANE
    nj = n_out // tn
    row = lambda i, j: (i, 0)
    fixed = lambda i, j: (0, 0)
    in_specs = [pl.BlockSpec((tm, D), row),
                pl.BlockSpec((1, D), fixed),
                pl.BlockSpec((None, 1, N_MOD * D), lambda i, j: (i // tpb, 0, 0)),
                pl.BlockSpec((1, N_MOD * D), fixed)]
    args = [hs, g.reshape(1, D), ml, mc]
    if mode == "glu":
        in_specs += [pl.BlockSpec((D, tn), lambda i, j: (0, j)),
                     pl.BlockSpec((D, tn), lambda i, j: (0, j + nj)),
                     pl.BlockSpec((1, tn), lambda i, j: (0, j)),
                     pl.BlockSpec((1, tn), lambda i, j: (0, j + nj))]
        b2 = bias.reshape(1, N)
        args += [w, w, b2, b2]
    else:
        in_specs.append(pl.BlockSpec((D, tn), lambda i, j: (0, j)))
        args.append(w)
        if mode == "rope":
            tab = pl.BlockSpec((tm, LANE), lambda i, j: (i % tpb, 0))
            in_specs += [tab, tab]
            args += list(rope)
    out_specs = pl.BlockSpec((tm, tn), lambda i, j: (i, j))
    out_shape = jax.ShapeDtypeStruct((M, n_out), ACT)
    if mode == "gla":
        in_specs.append(pl.BlockSpec((D, LANE), fixed))
        args.append(wa1)
        out_specs = [out_specs, pl.BlockSpec((tm, LANE), row)]
        out_shape = [out_shape, jax.ShapeDtypeStruct((M, LANE), ACT)]
    kern = functools.partial(_proj_kernel, mode=mode, tpb=tpb, n_ctx=n_ctx,
                             n_rope_tiles=n_rope_cols // tn, q_tiles=n_q_cols // tn,
                             q_scale=q_scale)
    return pl.pallas_call(
        kern,
        grid=(M // tm, nj),
        in_specs=in_specs,
        out_specs=out_specs,
        out_shape=out_shape,
        scratch_shapes=[pltpu.VMEM((tm, D), ACT)],
        compiler_params=_params(2),
        name="project_" + mode,
    )(*args)


def _mlp_kernel(h_ref, g_ref, ml_ref, mc_ref, w1_ref, w2_ref, o_ref, u_scr, *, tpb, n_ctx):
    i = pl.program_id(0)
    j = pl.program_id(1)
    tm, D = h_ref.shape
    first_tile = i % tpb == 0

    @pl.when(j == 0)
    def _():
        def store(u):
            u_scr[...] = u.astype(ACT)
        _norm_mod_store(h_ref, g_ref, ml_ref, mc_ref, 3, 4, first_tile, n_ctx, store)
        o_ref[...] = jnp.zeros_like(o_ref)

    hid = jnp.maximum(_dot(u_scr[...], w1_ref[...]), 0.0)
    o_ref[...] += _dot((hid * hid).astype(ACT), w2_ref[...])

    @pl.when(j == pl.num_programs(1) - 1)
    def _():
        gate = _mod_pick(ml_ref, mc_ref, 5, D, _ctx_rows(tm, n_ctx, first_tile))
        o_ref[...] = h_ref[...] + gate * o_ref[...]


def _mlp(hs, g, ml, mc, w1, w2, layer, *, L, n_ctx):
    M, D = hs.shape
    Hd = w1.shape[2]
    tm = _row_tile(L, n_ctx, 768)
    tpb = L // tm
    th = _col_tile(Hd, 1024)
    row = lambda i, j: (i, 0)
    fixed = lambda i, j: (0, 0)
    return pl.pallas_call(
        functools.partial(_mlp_kernel, tpb=tpb, n_ctx=n_ctx),
        grid=(M // tm, Hd // th),
        in_specs=[pl.BlockSpec((tm, D), row),
                  pl.BlockSpec((1, D), fixed),
                  pl.BlockSpec((None, 1, N_MOD * D), lambda i, j: (i // tpb, 0, 0)),
                  pl.BlockSpec((1, N_MOD * D), fixed),
                  pl.BlockSpec((None, D, th), lambda i, j: (layer, 0, j)),
                  pl.BlockSpec((None, th, D), lambda i, j: (layer, j, 0))],
        out_specs=pl.BlockSpec((tm, D), row),
        out_shape=jax.ShapeDtypeStruct((M, D), F32),
        scratch_shapes=[pltpu.VMEM((tm, D), ACT)],
        compiler_params=_params(2),
        name="mlp",
    )(hs, g.reshape(1, D), ml, mc, w1, w2)


def _out_kernel(*refs, mode, tpb, n_ctx, n_heads):
    if mode == "gla":
        h_ref, ml_ref, mc_ref, w_ref, of_ref, ob_ref, gt_ref, ng_ref, o_ref = refs
    else:
        h_ref, ml_ref, mc_ref, w_ref, aa_ref, ab_ref, o_ref = refs
    i = pl.program_id(0)
    tm, D = h_ref.shape
    ctx_row = _ctx_rows(tm, n_ctx, i % tpb == 0)
    if mode == "gla":
        o = of_ref[...].astype(F32) + ob_ref[...].astype(F32)
        dv = o.shape[1] // n_heads
        ng = ng_ref[...]
        o = jnp.concatenate([_rms(o[:, hh * dv:(hh + 1) * dv]) * ng for hh in range(n_heads)],
                            axis=1)
        gt = gt_ref[...].astype(F32)
        a = (o * (gt * _sigmoid(gt))).astype(ACT)
    else:
        t = i % tpb
        a = jnp.where(jnp.logical_and(t >= 2, t % 2 == 0), ab_ref[...], aa_ref[...])
    gate = _mod_pick(ml_ref, mc_ref, 2, D, ctx_row)
    o_ref[...] = h_ref[...] + gate * _dot(a, w_ref[...])


def _out_residual(hs, ml, mc, w, *, L, n_ctx, a=None, gla=None):
    M, D = hs.shape
    K = w.shape[0]
    tm = _row_tile(L, n_ctx, 256)
    tpb = L // tm
    row = lambda i: (i, 0)
    fixed = lambda i: (0, 0)
    in_specs = [pl.BlockSpec((tm, D), row),
                pl.BlockSpec((None, 1, N_MOD * D), lambda i: (i // tpb, 0, 0)),
                pl.BlockSpec((1, N_MOD * D), fixed),
                pl.BlockSpec((K, D), fixed)]
    args = [hs, ml, mc, w]
    if gla is not None:
        o_f, o_b, qkvg, norm_g = gla
        g_blk = qkvg.shape[1] // K - 1
        in_specs += [pl.BlockSpec((tm, K), row), pl.BlockSpec((tm, K), row),
                     pl.BlockSpec((tm, K), lambda i: (i, g_blk)),
                     pl.BlockSpec((1, norm_g.shape[0]), fixed)]
        args += [o_f, o_b, qkvg, norm_g.reshape(1, -1)]
        mode = "gla"
    else:
        assert tm == n_ctx and (tpb - 1) % 2 == 0
        na, nb = (tpb + 1) // 2, (tpb - 1) // 2
        in_specs += [pl.BlockSpec((tm, K), lambda i: ((i // tpb) * na + (i % tpb + 1) // 2, 0)),
                     pl.BlockSpec((tm, K), lambda i: (
                         (i // tpb) * nb + jnp.maximum((i % tpb) // 2 - 1, 0), 0))]
        args += list(a)
        mode = "pair"
    return pl.pallas_call(
        functools.partial(_out_kernel, mode=mode, tpb=tpb, n_ctx=n_ctx, n_heads=GLA_HEADS),
        grid=(M // tm,),
        in_specs=in_specs,
        out_specs=pl.BlockSpec((tm, D), row),
        out_shape=jax.ShapeDtypeStruct((M, D), F32),
        compiler_params=_params(1),
        name="out_" + mode,
    )(*args)


def _gla_direction(t_ref, wa2_ref, ba_ref, q_ref, k_ref, v_ref, o_ref, s_scr,
                   la_scr, qd_scr, qa_scr, qf_scr, kr_scr, km_scr, oi_scr, af_scr,
                   *, forward, q_scale):
    C = GLA_CHUNK
    blk = q_ref.shape[0]
    n = blk // C
    dk, dv = s_scr.shape
    order = list(range(n)) if forward else list(range(n - 1, -1, -1))
    last_row = C - 1 if forward else 0
    r = lax.broadcasted_iota(jnp.int32, (blk, blk), 0)
    c = lax.broadcasted_iota(jnp.int32, (blk, blk), 1)
    causal = (c <= r) if forward else (c >= r)
    shift = C.bit_length() - 1
    same_chunk = jnp.right_shift(r, shift) == jnp.right_shift(c, shift)
    tri = jnp.logical_and(causal, same_chunk).astype(ACT)

    z = _dot(t_ref[...], wa2_ref[...]) + ba_ref[...]
    g = (jnp.minimum(z, 0.0) - jnp.log(1.0 + jnp.exp(-jnp.abs(z)))) / GLA_GATE_NORM
    g_hi = g.astype(ACT)
    g_lo = (g - g_hi.astype(F32)).astype(ACT)
    la_scr[...] = _dot(tri, g_hi) + _dot(tri, g_lo)

    totals = []
    for ci in range(n):
        rows = pl.ds(ci * C, C)
        b = la_scr[rows, :]
        total = b[last_row:last_row + 1, :]
        totals.append(total)
        q = q_ref[rows, :].astype(F32) * q_scale
        k = k_ref[rows, :].astype(F32)
        q_dec = q * jnp.exp(b)
        qd_scr[rows, :] = q_dec
        qa_scr[rows, :] = q_dec.astype(ACT)
        km_scr[rows, :] = (k * jnp.exp(-b)).astype(ACT)
        kr_scr[rows, :] = k * jnp.exp(total - b)

    run = None
    for ci in order:
        rows = pl.ds(ci * C, C)
        q_dec = qd_scr[rows, :]
        qf_scr[rows, :] = (q_dec if run is None else q_dec * jnp.exp(run)).astype(ACT)
        run = totals[ci] if run is None else run + totals[ci]
    state = s_scr[...]
    oi_scr[...] = _dot(qf_scr[...], state.astype(ACT))

    def keys_at(pos, cj):
        gap = None
        for cm in order[order.index(cj) + 1:pos]:
            gap = totals[cm] if gap is None else gap + totals[cm]
        k_end = kr_scr[pl.ds(cj * C, C), :]
        return k_end if gap is None else k_end * jnp.exp(gap)

    for pos, ci in enumerate(order):
        rows = pl.ds(ci * C, C)
        keys = jnp.concatenate(
            [keys_at(pos, cj).astype(ACT) if cj in order[:pos] else km_scr[pl.ds(cj * C, C), :]
             for cj in range(n)], axis=0)
        att = _dot_nt(qa_scr[rows, :], keys)
        af_scr[rows, :] = jnp.where(causal[ci * C:(ci + 1) * C, :], att, 0.0).astype(ACT)

    k_fin = jnp.concatenate([keys_at(n, cj) for cj in range(n)], axis=0)
    both = _dot(jnp.concatenate([af_scr[...], k_fin.T.astype(ACT)], axis=0), v_ref[...])
    o_ref[...] = (both[:blk, :] + oi_scr[...]).astype(o_ref.dtype)
    decay_col = jnp.broadcast_to(jnp.exp(run), (LANE, dk)).T
    s_scr[...] = state * jnp.tile(decay_col, (1, dv // LANE)) + both[blk:, :]


def _gla_kernel(tf_ref, w2f_ref, baf_ref, qf_ref, kf_ref, vf_ref,
                tb_ref, w2b_ref, bab_ref, qb_ref, kb_ref, vb_ref,
                of_ref, ob_ref, sf_scr, sb_scr, *scratch, q_scale):
    @pl.when(pl.program_id(2) == 0)
    def _():
        sf_scr[...] = jnp.zeros_like(sf_scr)
        sb_scr[...] = jnp.zeros_like(sb_scr)

    heads, dk, dv = sf_scr.shape
    per = len(scratch) // (2 * heads)
    for hh in range(heads):
        kc, vc = pl.ds(hh * dk, dk), pl.ds(hh * dv, dv)
        _gla_direction(tf_ref, w2f_ref.at[:, kc], baf_ref.at[:, kc], qf_ref.at[:, kc],
                       kf_ref.at[:, kc], vf_ref.at[:, vc], of_ref.at[:, vc], sf_scr.at[hh],
                       *scratch[2 * hh * per:(2 * hh + 1) * per], forward=True, q_scale=q_scale)
        _gla_direction(tb_ref, w2b_ref.at[:, kc], bab_ref.at[:, kc], qb_ref.at[:, kc],
                       kb_ref.at[:, kc], vb_ref.at[:, vc], ob_ref.at[:, vc], sb_scr.at[hh],
                       *scratch[(2 * hh + 1) * per:(2 * hh + 2) * per], forward=False,
                       q_scale=q_scale)


def _gla_scan(qkvg, t, wa2, ba, *, B, L, n_ctx, dk, dv):
    M = qkvg.shape[0]
    H = GLA_HEADS
    hp = GLA_HEADS_PER_STEP
    blk = n_ctx
    nb = L // blk
    wk, wv = hp * dk, hp * dv
    kq = (H * dk) // wk
    kv = (2 * H * dk) // wv

    def fwd(n):
        return n

    def bwd(n):
        return jnp.where(n == 0, 0, nb - n)

    def direction_specs(d, order):
        rows = lambda b, n: b * nb + order(n)
        return [pl.BlockSpec((blk, LANE), lambda b, h, n: (rows(b, n), 0)),
                pl.BlockSpec((None, LANE, wk), lambda b, h, n: (d, 0, h)),
                pl.BlockSpec((None, 1, wk), lambda b, h, n: (d, 0, h)),
                pl.BlockSpec((blk, wk), lambda b, h, n: (rows(b, n), h)),
                pl.BlockSpec((blk, wk), lambda b, h, n: (rows(b, n), kq + h)),
                pl.BlockSpec((blk, wv), lambda b, h, n: (rows(b, n), kv + h))]

    out_spec = lambda order: pl.BlockSpec((blk, wv), lambda b, h, n: (b * nb + order(n), h))
    stage = [pltpu.VMEM((blk, dk), F32), pltpu.VMEM((blk, dk), F32), pltpu.VMEM((blk, dk), ACT),
             pltpu.VMEM((blk, dk), ACT), pltpu.VMEM((blk, dk), F32), pltpu.VMEM((blk, dk), ACT),
             pltpu.VMEM((blk, dv), F32), pltpu.VMEM((blk, blk), ACT)]
    state = pltpu.VMEM((hp, dk, dv), F32)
    return pl.pallas_call(
        functools.partial(_gla_kernel, q_scale=dk ** -0.5),
        grid=(B, H // hp, nb),
        in_specs=direction_specs(0, fwd) + direction_specs(1, bwd),
        out_specs=[out_spec(fwd), out_spec(bwd)],
        out_shape=[jax.ShapeDtypeStruct((M, H * dv), ACT)] * 2,
        scratch_shapes=[state, state] + stage * (2 * hp),
        compiler_params=_params(3),
        name="gla_scan",
    )(t, wa2, ba, qkvg, qkvg, qkvg, t, wa2, ba, qkvg, qkvg, qkvg)


def _conv_kernel(h_ref, ml_ref, mc_ref, x_ref, prev_ref, next_ref, dw_ref, dwb_ref,
                 lng_ref, lnb_ref, w_ref, b_ref, o_ref, xw_scr, y_scr, *, tpb):
    i = pl.program_id(0)
    tm, D = h_ref.shape
    t = i % tpb
    has_prev = t > 1
    has_next = jnp.logical_and(t > 0, t < tpb - 1)
    xw_scr[0, 0:CONV_HALO, :] = jnp.where(has_prev, prev_ref[...].astype(F32), 0.0)
    xw_scr[0, CONV_HALO:CONV_HALO + tm, :] = x_ref[...].astype(F32)
    xw_scr[0, CONV_HALO + tm:, :] = jnp.where(has_next, next_ref[...].astype(F32), 0.0)
    n_shift = tm + 2 * CONV_HALO - SUBLANE
    for r in range(1, SUBLANE):
        xw_scr[r, 0:n_shift, :] = xw_scr[0, r:r + n_shift, :]
    first = CONV_HALO - CONV_WIDTH // 2
    rb = min(tm, 128)

    def col_body(cb, carry):
        cols = pl.ds(pl.multiple_of(cb * LANE, LANE), LANE)
        for r0 in range(0, tm, rb):
            acc = jnp.zeros((rb, LANE), F32)
            for k in range(CONV_WIDTH):
                a, r = divmod(first + k, SUBLANE)
                tap = xw_scr[r, pl.ds(r0 + a * SUBLANE, rb), cols]
                acc = acc + tap * dw_ref[pl.ds(k, 1), cols]
            y_scr[pl.ds(r0, rb), cols] = acc + dwb_ref[:, cols]
        return carry

    lax.fori_loop(0, D // LANE, col_body, 0)
    y = y_scr[...]
    yc = y - jnp.mean(y, axis=-1, keepdims=True)
    yn = yc * lax.rsqrt(jnp.mean(yc * yc, axis=-1, keepdims=True) + EPS)
    yn = yn * lng_ref[...] + lnb_ref[...]
    a = (yn * _sigmoid(yn)).astype(ACT)
    gate = _mod_pick(ml_ref, mc_ref, 2, D, t == 0)
    o_ref[...] = h_ref[...] + gate * (_dot(a, w_ref[...]) + b_ref[...])


def _conv_out(hs, ml, mc, glu, dw, dwb, ln_g, ln_b, w2, b2, *, L, n_ctx):
    M, D = hs.shape
    tm = n_ctx
    tpb = L // tm
    hb = tm // CONV_HALO
    n_halo = M // CONV_HALO
    row = lambda i: (i, 0)
    fixed = lambda i: (0, 0)
    vec = pl.BlockSpec((1, D), fixed)
    return pl.pallas_call(
        functools.partial(_conv_kernel, tpb=tpb),
        grid=(M // tm,),
        in_specs=[pl.BlockSpec((tm, D), row),
                  pl.BlockSpec((None, 1, N_MOD * D), lambda i: (i // tpb, 0, 0)),
                  pl.BlockSpec((1, N_MOD * D), fixed),
                  pl.BlockSpec((tm, D), row),
                  pl.BlockSpec((CONV_HALO, D), lambda i: (jnp.maximum(i * hb - 1, 0), 0)),
                  pl.BlockSpec((CONV_HALO, D), lambda i: (jnp.minimum((i + 1) * hb, n_halo - 1), 0)),
                  pl.BlockSpec((CONV_WIDTH, D), fixed),
                  vec, vec, vec,
                  pl.BlockSpec((D, D), fixed),
                  vec],
        out_specs=pl.BlockSpec((tm, D), row),
        out_shape=jax.ShapeDtypeStruct((M, D), F32),
        scratch_shapes=[pltpu.VMEM((SUBLANE, tm + 2 * CONV_HALO, D), F32),
                        pltpu.VMEM((tm, D), F32)],
        compiler_params=_params(1),
        name="conv_out",
    )(hs, ml, mc, glu, glu, glu, dw, dwb.reshape(1, D), ln_g.reshape(1, D), ln_b.reshape(1, D),
      w2, b2.reshape(1, D))


def _attn_tile(q_ref, o_ref, k_ref, lq1_ref, lk1_ref, lq2_ref, lk2_ref, sg_ref,
               vtc_scr, vtl_scr, acc_scr, m_scr, l_scr, s0_scr, s1_scr, p0_scr, p1_scr,
               a0_scr, a1_scr, x0_scr, x1_scr, *, n_ctx, tk, lambda_init):
    n_lat_steps = vtl_scr.shape[0]

    def context_keys():
        vt = vtc_scr[...]
        for j in range(2):
            lanes = slice(j * DIFF_DK, (j + 1) * DIFF_DK)
            st = _dot_nt(k_ref[0:n_ctx, lanes], q_ref[:, lanes])
            m = jnp.max(st, axis=0, keepdims=True)
            p = jnp.exp2(st - m)
            l_scr[j] = jnp.sum(p, axis=0, keepdims=True)
            acc_scr[j] = _dot(vt, p.astype(ACT))
            m_scr[j] = m

    def finish():
        lam = (jnp.exp(jnp.sum(lq1_ref[...] * lk1_ref[...], axis=-1, keepdims=True))
               - jnp.exp(jnp.sum(lq2_ref[...] * lk2_ref[...], axis=-1, keepdims=True))
               + lambda_init)
        ot = acc_scr[0] * (1.0 / l_scr[0]) - lam * (acc_scr[1] * (1.0 / l_scr[1]))
        ot = ot * lax.rsqrt(jnp.mean(ot * ot, axis=0, keepdims=True) + EPS)
        ot = ot * (sg_ref[...] * (1.0 - lambda_init))
        o_ref[...] = ot.T.astype(o_ref.dtype)

    def scores(t, s_scr, x_scr):
        for j in range(2):
            lanes = slice(j * DIFF_DK, (j + 1) * DIFF_DK)
            part = None
            for c0 in range(0, tk, ATTN_CHUNK):
                rows = pl.ds(pl.multiple_of(n_ctx + t * tk + c0, ATTN_CHUNK), ATTN_CHUNK)
                st = _dot_nt(k_ref[rows, lanes], q_ref[:, lanes])
                s_scr[j, c0:c0 + ATTN_CHUNK, :] = st
                part = st if part is None else jnp.maximum(part, st)
            x_scr[j] = jnp.max(part, axis=0, keepdims=True)

    def softmax(s_scr, x_scr, p_scr, a_scr):
        for j in range(2):
            m_prev = m_scr[j]
            m_new = jnp.maximum(m_prev, x_scr[j])
            alpha = jnp.exp2(m_prev - m_new)
            part = None
            for c0 in range(0, tk, ATTN_CHUNK):
                p = jnp.exp2(s_scr[j, c0:c0 + ATTN_CHUNK, :] - m_new)
                p_scr[j, c0:c0 + ATTN_CHUNK, :] = p.astype(ACT)
                part = p if part is None else part + p
            l_scr[j] = alpha * l_scr[j] + jnp.sum(part, axis=0, keepdims=True)
            m_scr[j] = m_new
            a_scr[j] = alpha

    def values(t, p_scr, a_scr):
        vt = vtl_scr[t]
        for j in range(2):
            acc_scr[j] = a_scr[j] * acc_scr[j] + _dot(vt, p_scr[j])

    def latent_keys():
        scores(0, s0_scr, x0_scr)
        scores(1, s1_scr, x1_scr)
        softmax(s0_scr, x0_scr, p0_scr, a0_scr)

        def body(i, carry):
            t = 2 * i
            scores(t + 2, s0_scr, x0_scr)
            softmax(s1_scr, x1_scr, p1_scr, a1_scr)
            values(t, p0_scr, a0_scr)
            scores(t + 3, s1_scr, x1_scr)
            softmax(s0_scr, x0_scr, p0_scr, a0_scr)
            values(t + 1, p1_scr, a1_scr)
            return carry
        lax.fori_loop(0, n_lat_steps // 2 - 1, body, 0, unroll=True)
        softmax(s1_scr, x1_scr, p1_scr, a1_scr)
        values(n_lat_steps - 2, p0_scr, a0_scr)
        values(n_lat_steps - 1, p1_scr, a1_scr)

    return context_keys, latent_keys, finish


def _attn_kernel(qa_ref, qb_ref, k_ref, v_ref, lq1_ref, lk1_ref, lq2_ref, lk2_ref, sg_ref,
                 oa_ref, ob_ref, vtc_scr, vtl_scr, *scratch, n_ctx, tk, lambda_init):
    per = len(scratch) // 2
    shared = (k_ref, lq1_ref, lk1_ref, lq2_ref, lk2_ref, sg_ref, vtc_scr, vtl_scr)
    consts = dict(n_ctx=n_ctx, tk=tk, lambda_init=lambda_init)
    ctx_a, lat_a, fin_a = _attn_tile(qa_ref, oa_ref, *shared, *scratch[:per], **consts)
    ctx_b, lat_b, fin_b = _attn_tile(qb_ref, ob_ref, *shared, *scratch[per:], **consts)
    first = pl.program_id(2) == 0

    @pl.when(first)
    def _():
        vtc_scr[...] = v_ref[0:n_ctx, :].astype(F32).T.astype(ACT)

        def fill(t, carry):
            rows = pl.ds(pl.multiple_of(n_ctx + t * tk, n_ctx), tk)
            vtl_scr[t] = v_ref[rows, :].astype(F32).T.astype(ACT)
            return carry
        lax.fori_loop(0, vtl_scr.shape[0], fill, 0)
        ctx_a()
        fin_a()
        ob_ref[...] = jnp.zeros_like(ob_ref)

    @pl.when(jnp.logical_not(first))
    def _():
        ctx_a()
        lat_a()
        fin_a()
        ctx_b()
        lat_b()
        fin_b()


def _diff_attention(qkv, lq1, lk1, lq2, lk2, subln_g, *, B, L, n_ctx, lambda_init):
    M, N = qkv.shape
    width = 2 * DIFF_DK
    H = N // (3 * width)
    tq = n_ctx
    nq = L // tq
    assert (nq - 1) % 2 == 0
    n_pairs = (nq - 1) // 2
    tk = _row_tile(L - n_ctx, n_ctx, 512)
    n_steps = (L - n_ctx) // tk
    assert n_steps >= 2 and n_steps % 2 == 0
    stage = lambda dtype: pltpu.VMEM((2, tk, tq), dtype)
    stat = pltpu.VMEM((2, 1, tq), F32)
    per_tile = [pltpu.VMEM((2, DIFF_DV, tq), F32), stat, stat,
                stage(F32), stage(F32), stage(ACT), stage(ACT), stat, stat, stat, stat]
    fixed = lambda b, h, s: (0, 0)
    vec = pl.BlockSpec((1, DIFF_DK), fixed)
    tile_a = lambda s: jnp.maximum(2 * s - 1, 0)
    tile_b = lambda s: jnp.maximum(2 * s, 1)
    return pl.pallas_call(
        functools.partial(_attn_kernel, n_ctx=n_ctx, tk=tk, lambda_init=lambda_init),
        grid=(B, H, n_pairs + 1),
        in_specs=[pl.BlockSpec((tq, width), lambda b, h, s: (b * nq + tile_a(s), h)),
                  pl.BlockSpec((tq, width), lambda b, h, s: (b * nq + tile_b(s), h)),
                  pl.BlockSpec((L, width), lambda b, h, s: (b, H + h)),
                  pl.BlockSpec((L, DIFF_DV), lambda b, h, s: (b, 2 * H + h)),
                  vec, vec, vec, vec,
                  pl.BlockSpec((DIFF_DV, 1), fixed)],
        out_specs=[pl.BlockSpec((tq, DIFF_DV), lambda b, h, s: (b * (n_pairs + 1) + s, h)),
                   pl.BlockSpec((tq, DIFF_DV),
                                lambda b, h, s: (b * n_pairs + jnp.maximum(s - 1, 0), h))],
        out_shape=[jax.ShapeDtypeStruct((B * (n_pairs + 1) * tq, H * DIFF_DV), ACT),
                   jax.ShapeDtypeStruct((B * n_pairs * tq, H * DIFF_DV), ACT)],
        scratch_shapes=[pltpu.VMEM((DIFF_DV, n_ctx), ACT),
                        pltpu.VMEM((n_steps, DIFF_DV, tk), ACT)] + per_tile + per_tile,
        compiler_params=_params(3),
        name="diff_attention",
    )(qkv, qkv, qkv, qkv, lq1.reshape(1, -1), lk1.reshape(1, -1), lq2.reshape(1, -1),
      lk2.reshape(1, -1), subln_g.reshape(-1, 1))


def _rope_layout():
    quarter = DIFF_DK // 4
    idx = jnp.arange(DIFF_DK).reshape(4, quarter)
    return idx[jnp.array([0, 2, 1, 3])].reshape(-1)


def _rope_tables(n_ctx, n_lat):
    axis_dim = DIFF_DK // 2
    rows = n_lat // GRID_W
    pos_row = jnp.repeat(jnp.arange(rows, dtype=F32), GRID_W)
    pos_col = jnp.tile(jnp.arange(GRID_W, dtype=F32), rows)
    inv_freq = ROPE_BASE ** (-jnp.arange(0, axis_dim, 2, dtype=F32) / axis_dim)
    ang_r = pos_row[:, None] * inv_freq
    ang_c = pos_col[:, None] * inv_freq
    ang = jnp.concatenate([ang_r, ang_r, ang_c, ang_c], axis=-1)
    cos = jnp.concatenate([jnp.ones((n_ctx, DIFF_DK), F32), jnp.cos(ang)], axis=0)
    sin = jnp.concatenate([jnp.zeros((n_ctx, DIFF_DK), F32), jnp.sin(ang)], axis=0)
    first_half = (jnp.arange(DIFF_DK) % (axis_dim)) < axis_dim // 2
    order = _rope_layout()
    return cos[:, order], jnp.where(first_half, -sin, sin)[:, order]


def _final_kernel(h_ref, g_ref, o_ref):
    o_ref[...] = _rms(h_ref[...]) * g_ref[...]


def _final_norm(hs, g, *, B, L, n_ctx):
    D = hs.shape[1]
    tm = n_ctx
    skip = n_ctx // tm
    n_lat = L - n_ctx
    return pl.pallas_call(
        _final_kernel,
        grid=(B, n_lat // tm),
        in_specs=[pl.BlockSpec((None, tm, D), lambda b, i: (b, i + skip, 0)),
                  pl.BlockSpec((1, D), lambda b, i: (0, 0))],
        out_specs=pl.BlockSpec((None, tm, D), lambda b, i: (b, i, 0)),
        out_shape=jax.ShapeDtypeStruct((B, n_lat, D), F32),
        compiler_params=_params(2),
        name="final_norm",
    )(hs.reshape(B, L, D), g.reshape(1, D))


def kernel(x, c, ctx, c_ctx, mod_w, mod_b, norm_mix_g, norm_mlp_g, mlp_w1, mlp_w2, gla_w_in, gla_wa1_f, gla_wa2_f, gla_ba_f, gla_wa1_b, gla_wa2_b, gla_ba_b, gla_norm_g, gla_w_o, conv_w1, conv_b1, conv_dw, conv_dwb, conv_ln_g, conv_ln_b, conv_w2, conv_b2, diff_w_qkv, diff_lq1, diff_lk1, diff_lq2, diff_lk2, diff_subln_g, diff_w_o, final_g):
    B, n_lat, D = x.shape
    n_ctx = ctx.shape[1]
    L = n_ctx + n_lat
    depth = mod_w.shape[0]
    assert B + 1 <= MOD_ROWS and n_lat % n_ctx == 0 and n_ctx % GLA_CHUNK == 0
    dims = dict(L=L, n_ctx=n_ctx)

    hs = jnp.concatenate([ctx, x], axis=1).reshape(B * L, D)
    cvec = jnp.zeros((MOD_ROWS, D), F32).at[:B].set(c).at[B].set(c_ctx)
    mods = _mod_vectors(cvec, mod_w, mod_b)
    rope = _rope_tables(n_ctx, n_lat)
    w1_all, w2_all = mlp_w1.astype(ACT), mlp_w2.astype(ACT)

    for i in range(depth):
        ml = mods[i, :B][:, None, :]
        mc = mods[i, B:B + 1]
        kind, j = i % 3, i // 3
        if kind == 0:
            hk = gla_wa2_f.shape[2]
            dk, dv = hk // GLA_HEADS, gla_w_o.shape[1] // GLA_HEADS
            wa1 = jnp.zeros((D, LANE), F32).at[:, :GLA_RANK].set(gla_wa1_f[j])
            wa1 = wa1.at[:, GLA_RANK:2 * GLA_RANK].set(gla_wa1_b[j])
            wa2 = jnp.zeros((2, LANE, hk), F32).at[0, :GLA_RANK].set(gla_wa2_f[j])
            wa2 = wa2.at[1, GLA_RANK:2 * GLA_RANK].set(gla_wa2_b[j])
            ba = jnp.stack([gla_ba_f[j], gla_ba_b[j]])[:, None, :]
            qkvg, t = _project(hs, norm_mix_g[i], ml, mc, gla_w_in[j].astype(ACT), mode="gla",
                               wa1=wa1.astype(ACT), **dims)
            o_f, o_b = _gla_scan(qkvg, t, wa2.astype(ACT), ba, B=B, dk=dk, dv=dv, **dims)
            hs = _out_residual(hs, ml, mc, gla_w_o[j].astype(ACT),
                               gla=(o_f, o_b, qkvg, gla_norm_g[j]), **dims)
        elif kind == 1:
            glu = _project(hs, norm_mix_g[i], ml, mc, conv_w1[j].astype(ACT), mode="glu",
                           bias=conv_b1[j], **dims)
            hs = _conv_out(hs, ml, mc, glu, conv_dw[j], conv_dwb[j], conv_ln_g[j], conv_ln_b[j],
                           conv_w2[j].astype(ACT), conv_b2[j], **dims)
        else:
            lambda_init = 0.8 - 0.6 * math.exp(-0.3 * i)
            n_qk = diff_w_qkv.shape[2] // 3
            w_qk = diff_w_qkv[j][:, :2 * n_qk].reshape(D, -1, DIFF_DK)[:, :, _rope_layout()]
            w_qkv = jnp.concatenate([w_qk.reshape(D, 2 * n_qk), diff_w_qkv[j][:, 2 * n_qk:]], axis=1)
            qkv = _project(hs, norm_mix_g[i], ml, mc, w_qkv.astype(ACT), mode="rope",
                           rope=rope, n_rope_cols=2 * n_qk, n_q_cols=n_qk,
                           q_scale=DIFF_DK ** -0.5 * math.log2(math.e), **dims)
            att = _diff_attention(qkv, diff_lq1[j], diff_lk1[j], diff_lq2[j], diff_lk2[j],
                                  diff_subln_g[j], B=B, lambda_init=lambda_init, **dims)
            hs = _out_residual(hs, ml, mc, diff_w_o[j].astype(ACT), a=att, **dims)
        hs = _mlp(hs, norm_mlp_g[i], ml, mc, w1_all, w2_all, i, **dims)
    return _final_norm(hs, final_g, B=B, L=L, n_ctx=n_ctx)
```
